```python
import math
import jax, jax.numpy as jnp
from jax import lax
import numpy as np

D_MODEL = 2048
BATCH = 16
SEQ = 2048
DEPTH = 1

GRID_W = 64
CTX_LEN = 256
D_SSM = 1024
GROUP_CH = 16
SSM_GROUPS = D_SSM // GROUP_CH
SSM_STATE = 64
N_DIR = 2
DT_MIN = 1e-3
DT_MAX = 1e-1
N_HEADS = 8
HEAD_DIM = 64
V_DIM = 2 * HEAD_DIM
D_QK = N_HEADS * 2 * HEAD_DIM
D_ATTN = N_HEADS * V_DIM
ATTN_SCALE = HEAD_DIM ** -0.5
ROPE_AXIS_DIM = HEAD_DIM // 2
ROPE_BASE = 10000.0
Q_BLOCK = 128
D_FF = ((8 * D_MODEL + 3 * 256 - 1) // (3 * 256)) * 256
D_IN = D_SSM + 2 * D_QK + D_ATTN + 2 * D_MODEL
IN_SPLITS = (D_SSM, D_SSM + D_QK, D_SSM + 2 * D_QK, D_SSM + 2 * D_QK + D_ATTN,
             D_SSM + 2 * D_QK + D_ATTN + D_MODEL)
EPS = 1e-6

kernel_name = "hybrid_s5_diffattn_dit_block"


def rms_norm(x, g):
    xf = x.astype(jnp.float32)
    y = xf * lax.rsqrt(jnp.mean(xf * xf, axis=-1, keepdims=True) + EPS)
    return (y * g.astype(jnp.float32)).astype(x.dtype)


def modulate(h, shift, scale):
    return h * (1.0 + scale) + shift


def axial_rope_tables(rows):
    row = jnp.repeat(jnp.arange(rows, dtype=jnp.float32), GRID_W)
    col = jnp.tile(jnp.arange(GRID_W, dtype=jnp.float32), rows)
    inv = 1.0 / (ROPE_BASE ** (jnp.arange(0, ROPE_AXIS_DIM, 2, dtype=jnp.float32) / ROPE_AXIS_DIM))
    ang_r = (row[:, None] * inv)[:, None, None, :]
    ang_c = (col[:, None] * inv)[:, None, None, :]
    return jnp.cos(ang_r), jnp.sin(ang_r), jnp.cos(ang_c), jnp.sin(ang_c)


def rope_half(x, cos, sin):
    x1, x2 = jnp.split(x, 2, axis=-1)
    return jnp.concatenate([x1 * cos - x2 * sin, x2 * cos + x1 * sin], axis=-1)


def apply_axial_rope(x, tables):
    cos_r, sin_r, cos_c, sin_c = tables
    xr, xc = jnp.split(x.astype(jnp.float32), 2, axis=-1)
    out = jnp.concatenate([rope_half(xr, cos_r, sin_r), rope_half(xc, cos_c, sin_c)], axis=-1)
    return out.astype(x.dtype)


def s5_discretise(a_re, a_im, log_dt, b_re, b_im):
    lam = lax.complex(a_re.astype(jnp.float32), a_im.astype(jnp.float32))
    dt = jnp.exp(log_dt.astype(jnp.float32))[:, None]
    lam_bar = jnp.exp(lam * dt)
    b = lax.complex(b_re.astype(jnp.float32), b_im.astype(jnp.float32))
    b_bar = ((lam_bar - 1.0) / lam)[..., None] * b
    return lam_bar, b_bar


def linear_scan(lam_bar, bu):
    a = jnp.broadcast_to(lam_bar, (1, bu.shape[1]) + lam_bar.shape)

    def combine(e1, e2):
        a1, b1 = e1
        a2, b2 = e2
        return a2 * a1, a2 * b1 + b2

    _, states = lax.associative_scan(combine, (a, bu), axis=1)
    return states


def s5_glu(y, w_glu, b_glu, dtype):
    z = jax.nn.gelu(y)
    return (z * jax.nn.sigmoid(z @ w_glu + b_glu)).astype(dtype)


def s5_branch(u_c, u_l, a_re, a_im, log_dt, b_re, b_im, c_re, c_im, d_skip, w_glu, b_glu, with_ctx):
    bsz, lc, _ = u_c.shape
    n_lat = u_l.shape[1]
    uc = u_c.astype(jnp.float32).reshape(bsz, lc, SSM_GROUPS, GROUP_CH)
    ul = u_l.astype(jnp.float32).reshape(bsz, n_lat, SSM_GROUPS, GROUP_CH)
    ys_c, ys_l = [], []
    for direction in range(N_DIR):
        rev = direction == 1
        lam_bar, b_bar = s5_discretise(a_re[direction], a_im[direction], log_dt[direction],
                                       b_re[direction], b_im[direction])
        c_mat = lax.complex(c_re[direction].astype(jnp.float32), c_im[direction].astype(jnp.float32))
        uc_d = jnp.flip(uc, axis=1) if rev else uc
        ul_d = jnp.flip(ul, axis=1) if rev else ul
        bu_c = jnp.einsum('blgc,gpc->blgp', uc_d.astype(jnp.complex64), b_bar)
        st_c = linear_scan(lam_bar, bu_c)
        bu_l = jnp.einsum('blgc,gpc->blgp', ul_d.astype(jnp.complex64), b_bar)
        bu_l = bu_l.at[:, 0].add(lam_bar * st_c[:, -1])
        st_l = linear_scan(lam_bar, bu_l)
        yl = jnp.real(jnp.einsum('blgp,gcp->blgc', st_l, c_mat))
        ys_l.append(jnp.flip(yl, axis=1) if rev else yl)
        if with_ctx:
            yc = jnp.real(jnp.einsum('blgp,gcp->blgc', st_c, c_mat))
            ys_c.append(jnp.flip(yc, axis=1) if rev else yc)
    d = d_skip.astype(jnp.float32).reshape(SSM_GROUPS, GROUP_CH)
    y_l = (ys_l[0] + ys_l[1] + d * ul).reshape(bsz, n_lat, D_SSM)
    y_l = s5_glu(y_l, w_glu, b_glu, u_l.dtype)
    y_c = None
    if with_ctx:
        y_c = (ys_c[0] + ys_c[1] + d * uc).reshape(bsz, lc, D_SSM)
        y_c = s5_glu(y_c, w_glu, b_glu, u_c.dtype)
    return y_c, y_l


def diff_attention(q, k, v, lam):
    s = jnp.einsum('bqhmd,bkhmd->bhmqk', q, k).astype(jnp.float32) * ATTN_SCALE
    p = jax.nn.softmax(s, axis=-1)
    p = p[:, :, 0] - lam * p[:, :, 1]
    return jnp.einsum('bhqk,bkhe->bqhe', p.astype(v.dtype), v)


def blocked_diff_attention(q, k, v, lam):
    bsz, n_lat, heads = q.shape[:3]
    nb = n_lat // Q_BLOCK
    qb = jnp.moveaxis(q.reshape(bsz, nb, Q_BLOCK, heads, 2, HEAD_DIM), 1, 0)
    ob = lax.map(lambda blk: diff_attention(blk, k, v, lam), qb)
    return jnp.moveaxis(ob, 0, 1).reshape(bsz, n_lat, heads, V_DIM)


def head_out(o, g, lam_init):
    return (rms_norm(o, g) * (1.0 - lam_init)).reshape(o.shape[0], o.shape[1], D_ATTN)


def branch_merge(y_ssm, o_attn, g_s, g_a, w_ps, w_pa, w_o):
    merged = jax.nn.sigmoid(g_s) * (y_ssm @ w_ps) + jax.nn.sigmoid(g_a) * (o_attn @ w_pa)
    return merged @ w_o


def swiglu(h, w_in, w_out):
    gate, up = jnp.split(h @ w_in, 2, axis=-1)
    return (jax.nn.silu(gate) * up) @ w_out


def setup_inputs(seed: int = 0) -> dict:
    key = jax.random.key(seed)
    ks = jax.random.split(key, 32)
    f32 = jnp.float32

    def nrm(k, shape, s):
        return jax.random.normal(k, shape, f32) * s

    ssm_shape = (DEPTH, N_DIR, SSM_GROUPS, SSM_STATE)
    return {
        "x": nrm(ks[0], (BATCH, SEQ, D_MODEL), 1.0),
        "c": nrm(ks[1], (BATCH, D_MODEL), 1.0),
        "ctx": nrm(ks[2], (BATCH, CTX_LEN, D_MODEL), 1.0),
        "c_ctx": nrm(ks[3], (D_MODEL,), 1.0),
        "w_ada": nrm(ks[4], (DEPTH, D_MODEL, 6 * D_MODEL), 0.5 * D_MODEL ** -0.5),
        "b_ada": nrm(ks[5], (DEPTH, 6 * D_MODEL), 0.02),
        "norm1_g": 1.0 + nrm(ks[6], (DEPTH, D_MODEL), 0.02),
        "w_in": nrm(ks[7], (DEPTH, D_MODEL, D_IN), D_MODEL ** -0.5),
        "ssm_a_re": -0.5 + nrm(ks[8], ssm_shape, 0.01),
        "ssm_a_im": jnp.pi * jnp.arange(SSM_STATE, dtype=f32) + nrm(ks[9], ssm_shape, 0.01),
        "ssm_log_dt": jax.random.uniform(ks[10], (DEPTH, N_DIR, SSM_GROUPS), f32,
                                         math.log(DT_MIN), math.log(DT_MAX)),
        "ssm_b_re": nrm(ks[11], ssm_shape + (GROUP_CH,), (2 * GROUP_CH) ** -0.5),
        "ssm_b_im": nrm(ks[12], ssm_shape + (GROUP_CH,), (2 * GROUP_CH) ** -0.5),
        "ssm_c_re": nrm(ks[13], (DEPTH, N_DIR, SSM_GROUPS, GROUP_CH, SSM_STATE), (2 * SSM_STATE) ** -0.5),
        "ssm_c_im": nrm(ks[14], (DEPTH, N_DIR, SSM_GROUPS, GROUP_CH, SSM_STATE), (2 * SSM_STATE) ** -0.5),
        "ssm_d": nrm(ks[15], (DEPTH, D_SSM), 1.0),
        "w_glu": nrm(ks[16], (DEPTH, D_SSM, D_SSM), D_SSM ** -0.5),
        "b_glu": nrm(ks[17], (DEPTH, D_SSM), 0.02),
        "lambda_q1": nrm(ks[18], (DEPTH, HEAD_DIM), 0.1),
        "lambda_k1": nrm(ks[19], (DEPTH, HEAD_DIM), 0.1),
        "lambda_q2": nrm(ks[20], (DEPTH, HEAD_DIM), 0.1),
        "lambda_k2": nrm(ks[21], (DEPTH, HEAD_DIM), 0.1),
        "subln_g": 1.0 + nrm(ks[22], (DEPTH, V_DIM), 0.02),
        "w_proj_ssm": nrm(ks[23], (DEPTH, D_SSM, D_MODEL), D_SSM ** -0.5),
        "w_proj_attn": nrm(ks[24], (DEPTH, D_ATTN, D_MODEL), D_ATTN ** -0.5),
        "w_out": nrm(ks[25], (DEPTH, D_MODEL, D_MODEL), D_MODEL ** -0.5),
        "norm2_g": 1.0 + nrm(ks[26], (DEPTH, D_MODEL), 0.02),
        "w_ffn_in": nrm(ks[27], (DEPTH, D_MODEL, 2 * D_FF), D_MODEL ** -0.5),
        "w_ffn_out": nrm(ks[28], (DEPTH, D_FF, D_MODEL), D_FF ** -0.5),
        "norm_f_g": 1.0 + nrm(ks[29], (D_MODEL,), 0.02),
    }


def reference(x, c, ctx, c_ctx, w_ada, b_ada, norm1_g, w_in, ssm_a_re, ssm_a_im, ssm_log_dt,
              ssm_b_re, ssm_b_im, ssm_c_re, ssm_c_im, ssm_d, w_glu, b_glu,
              lambda_q1, lambda_k1, lambda_q2, lambda_k2, subln_g,
              w_proj_ssm, w_proj_attn, w_out, norm2_g, w_ffn_in, w_ffn_out, norm_f_g):
    bsz, n_lat, _ = x.shape
    n_ctx = ctx.shape[1]
    rows = n_lat // GRID_W
    rope = axial_rope_tables(rows)
    silu_c = jax.nn.silu(c)
    silu_cc = jax.nn.silu(c_ctx)

    for li in range(DEPTH):
        with_ctx = li < DEPTH - 1
        lam_init = 0.8 - 0.6 * math.exp(-0.3 * li)
        mod = silu_c @ w_ada[li] + b_ada[li]
        mod_c = silu_cc @ w_ada[li] + b_ada[li]
        sh1, sc1, g1, sh2, sc2, g2 = jnp.split(mod[:, None, :], 6, axis=-1)
        sh1c, sc1c, g1c, sh2c, sc2c, g2c = jnp.split(mod_c, 6, axis=-1)

        h = modulate(rms_norm(x, norm1_g[li]), sh1, sc1)
        hc = modulate(rms_norm(ctx, norm1_g[li]), sh1c, sc1c)
        u, q, k, v, g_s, g_a = jnp.split(h @ w_in[li], IN_SPLITS, axis=-1)
        uc, qc, kc, vc, g_sc, g_ac = jnp.split(hc @ w_in[li], IN_SPLITS, axis=-1)

        y_ssm_c, y_ssm = s5_branch(uc, u, ssm_a_re[li], ssm_a_im[li], ssm_log_dt[li],
                                   ssm_b_re[li], ssm_b_im[li], ssm_c_re[li], ssm_c_im[li],
                                   ssm_d[li], w_glu[li], b_glu[li], with_ctx)

        lq1 = lambda_q1[li].astype(jnp.float32)
        lk1 = lambda_k1[li].astype(jnp.float32)
        lq2 = lambda_q2[li].astype(jnp.float32)
        lk2 = lambda_k2[li].astype(jnp.float32)
        lam = jnp.exp(jnp.sum(lq1 * lk1)) - jnp.exp(jnp.sum(lq2 * lk2)) + lam_init
        q = apply_axial_rope(q.reshape(bsz, n_lat, N_HEADS, 2, HEAD_DIM), rope)
        k = apply_axial_rope(k.reshape(bsz, n_lat, N_HEADS, 2, HEAD_DIM), rope)
        v = v.reshape(bsz, n_lat, N_HEADS, V_DIM)
        kc = kc.reshape(bsz, n_ctx, N_HEADS, 2, HEAD_DIM)
        vc = vc.reshape(bsz, n_ctx, N_HEADS, V_DIM)
        k_all = jnp.concatenate([k, kc], axis=1)
        v_all = jnp.concatenate([v, vc], axis=1)
        o = head_out(blocked_diff_attention(q, k_all, v_all, lam), subln_g[li], lam_init)

        x_mid = x + g1 * branch_merge(y_ssm, o, g_s, g_a, w_proj_ssm[li], w_proj_attn[li], w_out[li])

        if with_ctx:
            oc = diff_attention(qc.reshape(bsz, n_ctx, N_HEADS, 2, HEAD_DIM), kc, vc, lam)
            oc = head_out(oc, subln_g[li], lam_init)
            ctx = ctx + g1c * branch_merge(y_ssm_c, oc, g_sc, g_ac, w_proj_ssm[li],
                                           w_proj_attn[li], w_out[li])
            ctx = ctx + g2c * swiglu(modulate(rms_norm(ctx, norm2_g[li]), sh2c, sc2c),
                                     w_ffn_in[li], w_ffn_out[li])

        x = x_mid + g2 * swiglu(modulate(rms_norm(x_mid, norm2_g[li]), sh2, sc2),
                                w_ffn_in[li], w_ffn_out[li])

    return rms_norm(x, norm_f_g)
```

```python
import functools
import math

import jax
import jax.numpy as jnp
from jax import lax
from jax.experimental import pallas as pl
from jax.experimental.pallas import tpu as pltpu

F32 = jnp.float32
BF16 = jnp.bfloat16

D_MODEL = 2048
GRID_W = 64
D_SSM = 1024
GROUP_CH = 16
SSM_GROUPS = D_SSM // GROUP_CH
SSM_STATE = 64
N_HEADS = 8
HEAD_DIM = 64
V_DIM = 2 * HEAD_DIM
D_QK = N_HEADS * 2 * HEAD_DIM
D_ATTN = N_HEADS * V_DIM
ATTN_SCALE = HEAD_DIM ** -0.5
ROPE_AXIS_DIM = HEAD_DIM // 2
ROPE_BASE = 10000.0
D_FF = 5632
D_IN = D_SSM + 2 * D_QK + D_ATTN + 2 * D_MODEL
EPS = 1e-6
LAM_INIT = 0.8 - 0.6 * math.exp(-0.3 * 0)

LANES = 128
SSM_CHUNK = 16
CHUNK_W = SSM_CHUNK * GROUP_CH
STATE_W = 2 * SSM_STATE
VMEM_LIMIT = 56 * 1024 * 1024


def _cparams(*sem):
    return pltpu.CompilerParams(dimension_semantics=sem, vmem_limit_bytes=VMEM_LIMIT)


def _dot(a, b):
    return jnp.dot(a, b, preferred_element_type=F32)


def _dot_nt(a, b, precision=None):
    return lax.dot_general(a, b, (((1,), (1,)), ((), ())), preferred_element_type=F32,
                           precision=precision)


def _sigmoid(x):
    return 1.0 / (1.0 + jnp.exp(-x))


def _adaln_kernel(c_ref, w_ref, b_ref, o_ref):
    c = c_ref[...]
    s = (c * _sigmoid(c)).astype(BF16)
    o_ref[...] = _dot(s, w_ref[...].astype(BF16)) + b_ref[...]


def _adaln(cc, w_ada, b_ada):
    rows, d = cc.shape
    n = w_ada.shape[1]
    tn = 1024
    return pl.pallas_call(
        _adaln_kernel,
        grid=(n // tn,),
        in_specs=[pl.BlockSpec((rows, d), lambda j: (0, 0)),
                  pl.BlockSpec((d, tn), lambda j: (0, j)),
                  pl.BlockSpec((1, tn), lambda j: (0, j))],
        out_specs=pl.BlockSpec((rows, tn), lambda j: (0, j)),
        out_shape=jax.ShapeDtypeStruct((rows, n), F32),
        compiler_params=_cparams("arbitrary"),
        name="adaln",
    )(cc, w_ada, b_ada.reshape(1, n))


def _norm_mod(x, g, shift, scale):
    ms = jnp.mean(x * x, axis=-1, keepdims=True)
    y = x * lax.rsqrt(ms + EPS) * g
    return y * (1.0 + scale) + shift


def _store_heads(ref, val):
    for h in range(N_HEADS):
        ref[h] = val[:, h * LANES:(h + 1) * LANES].astype(ref.dtype)


def _rope(acc, cos, sin_a, sin_b, scale):
    outs = []
    for h in range(acc.shape[1] // LANES):
        a = acc[:, h * LANES:(h + 1) * LANES]
        r = a * cos + pltpu.roll(a, LANES - 16, 1) * sin_a + pltpu.roll(a, 16, 1) * sin_b
        outs.append(r * scale if scale != 1.0 else r)
    return outs


def _inproj_lat_kernel(x_ref, mod_ref, g_ref, w_ref, cos_ref, sa_ref, sb_ref,
                       u_ref, q_ref, k_ref, v_ref, gs_ref, ga_ref, h_ref):
    j = pl.program_id(1)

    @pl.when(j == 0)
    def _():
        h_ref[...] = _norm_mod(x_ref[...], g_ref[...], mod_ref[0:1, :], mod_ref[1:2, :]).astype(BF16)

    acc = _dot(h_ref[...], w_ref[...])
    half = D_MODEL // 2

    @pl.when(j == 0)
    def _():
        u_ref[...] = acc.astype(BF16)

    @pl.when(j == 1)
    def _():
        for h, r in enumerate(_rope(acc, cos_ref[...], sa_ref[...], sb_ref[...], ATTN_SCALE)):
            q_ref[h] = r.astype(BF16)

    @pl.when(j == 2)
    def _():
        for h, r in enumerate(_rope(acc, cos_ref[...], sa_ref[...], sb_ref[...], 1.0)):
            k_ref[h] = r.astype(BF16)

    @pl.when(j == 3)
    def _():
        _store_heads(v_ref, acc)

    for jj, (ref, lo) in enumerate(((gs_ref, 0), (gs_ref, half), (ga_ref, 0), (ga_ref, half))):
        @pl.when(j == 4 + jj)
        def _(ref=ref, lo=lo):
            ref[:, lo:lo + half] = _sigmoid(acc).astype(BF16)


def _inproj_ctx_kernel(x_ref, mod_ref, g_ref, w_ref, u_ref, k_ref, v_ref, h_ref):
    j = pl.program_id(1)

    @pl.when(j == 0)
    def _():
        h_ref[...] = _norm_mod(x_ref[...], g_ref[...], mod_ref[0:1, :], mod_ref[1:2, :]).astype(BF16)

    acc = _dot(h_ref[...], w_ref[...])

    @pl.when(j == 0)
    def _():
        u_ref[...] = acc.astype(BF16)

    @pl.when(j == 1)
    def _():
        _store_heads(k_ref, acc)

    @pl.when(j == 2)
    def _():
        _store_heads(v_ref, acc)


def _rope_tables(n_lat):
    pos = jnp.arange(n_lat, dtype=jnp.int32)
    row = (pos // GRID_W).astype(F32)
    col = (pos % GRID_W).astype(F32)
    inv = 1.0 / (ROPE_BASE ** (jnp.arange(0, ROPE_AXIS_DIM, 2, dtype=F32) / ROPE_AXIS_DIM))
    lane = jnp.arange(LANES)
    d = lane % HEAD_DIM
    freq = inv[d % (ROPE_AXIS_DIM // 2)]
    ang = jnp.where(d < ROPE_AXIS_DIM, row[:, None], col[:, None]) * freq[None, :]
    first = (lane % ROPE_AXIS_DIM) < (ROPE_AXIS_DIM // 2)
    cos, sin = jnp.cos(ang), jnp.sin(ang)
    return cos, jnp.where(first, -sin, 0.0), jnp.where(first, 0.0, sin)


def _inproj_lat(x2d, mod, norm_g, w_in, n_lat):
    t = x2d.shape[0]
    tm = min(512, n_lat)
    tn = D_SSM
    tiles_per_b = n_lat // tm
    cos, sa, sb = _rope_tables(n_lat)
    row = lambda i, j: (i, 0)
    hm = lambda i, j: (0, i, 0)
    tab = pl.BlockSpec((tm, LANES), lambda i, j: (i % tiles_per_b, 0))
    return pl.pallas_call(
        _inproj_lat_kernel,
        grid=(t // tm, D_IN // tn),
        in_specs=[pl.BlockSpec((tm, D_MODEL), row),
                  pl.BlockSpec((None, 6, D_MODEL), lambda i, j: (i // tiles_per_b, 0, 0)),
                  pl.BlockSpec((1, D_MODEL), lambda i, j: (0, 0)),
                  pl.BlockSpec((D_MODEL, tn), lambda i, j: (0, j)),
                  tab, tab, tab],
        out_specs=[pl.BlockSpec((tm, D_SSM), row),
                   pl.BlockSpec((N_HEADS, tm, LANES), hm),
                   pl.BlockSpec((N_HEADS, tm, LANES), hm),
                   pl.BlockSpec((N_HEADS, tm, LANES), hm),
                   pl.BlockSpec((tm, D_MODEL), row),
                   pl.BlockSpec((tm, D_MODEL), row)],
        out_shape=[jax.ShapeDtypeStruct((t, D_SSM), BF16),
                   jax.ShapeDtypeStruct((N_HEADS, t, LANES), BF16),
                   jax.ShapeDtypeStruct((N_HEADS, t, LANES), BF16),
                   jax.ShapeDtypeStruct((N_HEADS, t, LANES), BF16),
                   jax.ShapeDtypeStruct((t, D_MODEL), BF16),
                   jax.ShapeDtypeStruct((t, D_MODEL), BF16)],
        scratch_shapes=[pltpu.VMEM((tm, D_MODEL), BF16)],
        compiler_params=_cparams("arbitrary", "arbitrary"),
        name="inproj_lat",
    )(x2d, mod, norm_g, w_in, cos, sa, sb)


def _inproj_ctx(c2d, mod_c, norm_g, w_in, n_ctx):
    t = c2d.shape[0]
    tm = min(512, t)
    tn = D_SSM
    row = lambda i, j: (i, 0)
    hm = lambda i, j: (0, i, 0)
    return pl.pallas_call(
        _inproj_ctx_kernel,
        grid=(t // tm, 3),
        in_specs=[pl.BlockSpec((tm, D_MODEL), row),
                  pl.BlockSpec((None, 6, D_MODEL), lambda i, j: (0, 0, 0)),
                  pl.BlockSpec((1, D_MODEL), lambda i, j: (0, 0)),
                  pl.BlockSpec((D_MODEL, tn), lambda i, j: (0, j + jnp.minimum(j, 1)))],
        out_specs=[pl.BlockSpec((tm, D_SSM), row),
                   pl.BlockSpec((N_HEADS, tm, LANES), hm),
                   pl.BlockSpec((N_HEADS, tm, LANES), hm)],
        out_shape=[jax.ShapeDtypeStruct((t, D_SSM), BF16),
                   jax.ShapeDtypeStruct((N_HEADS, t, LANES), BF16),
                   jax.ShapeDtypeStruct((N_HEADS, t, LANES), BF16)],
        scratch_shapes=[pltpu.VMEM((tm, D_MODEL), BF16)],
        compiler_params=_cparams("arbitrary", "arbitrary"),
        name="inproj_ctx",
    )(c2d, mod_c, norm_g, w_in)


def _ssm_param_kernel(a_re_ref, a_im_ref, ldt_ref, bt_ref, btsw_ref, cr_ref, ci_ref, d_ref,
                      m_ref, w_ref, v_ref, a_ref):
    tc = SSM_CHUNK
    lane = lax.broadcasted_iota(jnp.int32, (1, STATE_W), 1)
    lo = lane < SSM_STATE
    kts = []
    for d in range(2):
        dt = jnp.exp(ldt_ref[d:d + 1, :])
        lre = a_re_ref[d:d + 1, :] * dt
        lim = a_im_ref[d:d + 1, :] * dt
        kk = lax.broadcasted_iota(jnp.int32, (tc + 1, STATE_W), 0).astype(F32)
        mag = jnp.exp(kk * lre)
        c2 = mag * jnp.cos(kk * lim)
        s2 = mag * jnp.sin(kk * lim)
        xr = c2[1:2] - 1.0
        xi = s2[1:2]
        ar = a_re_ref[d:d + 1, :]
        ai = a_im_ref[d:d + 1, :]
        den = ar * ar + ai * ai
        er = (xr * ar + xi * ai) / den
        ei = (xi * ar - xr * ai) / den
        e2 = jnp.where(lo, -ei, ei)
        bt = bt_ref[d]
        btsw = btsw_ref[d]
        bb = er * bt + e2 * btsw
        bbsw = er * btsw - e2 * bt
        cdr = cr_ref[d]
        cdi = ci_ref[d]
        p2 = jnp.where(lo, -s2, s2)
        q1 = jnp.where(lo, c2, -s2)
        q2 = jnp.where(lo, -s2, -c2)
        r1 = jnp.where(lo, c2, s2)
        r2 = jnp.where(lo, -s2, c2)
        cl = []
        for s in range(tc):
            kw = tc - 1 - s if d == 0 else s
            w_ref[d, s] = (c2[kw:kw + 1] * bb + p2[kw:kw + 1] * bbsw).astype(w_ref.dtype)
            kv = s + 1 if d == 0 else tc - s
            v_ref[d, s] = (cdr * q1[kv:kv + 1] + cdi * q2[kv:kv + 1]).astype(v_ref.dtype)
            kl = s if d == 0 else tc - 1 - s
            cl.append(cdr * r1[kl:kl + 1] + cdi * r2[kl:kl + 1])
        clcat = jnp.concatenate(cl, axis=0)
        bneg = jnp.where(lo, bb, -bb)
        kts.append(_dot_nt(bneg, clcat, precision=lax.Precision.HIGHEST))
        a_ref[d, 0:1, :] = c2[tc:tc + 1]
        a_ref[d, 1:2, :] = p2[tc:tc + 1]
    lane_w = lax.broadcasted_iota(jnp.int32, (GROUP_CH, CHUNK_W), 1)
    row_w = lax.broadcasted_iota(jnp.int32, (GROUP_CH, CHUNK_W), 0)
    dcol = d_ref[...]
    for s in range(tc):
        fwd = kts[0] if s == 0 else pltpu.roll(kts[0], GROUP_CH * s, 1)
        fwd = jnp.where(lane_w >= GROUP_CH * s, fwd, 0.0)
        sh = (CHUNK_W - GROUP_CH * (tc - 1 - s)) % CHUNK_W
        rev = kts[1] if sh == 0 else pltpu.roll(kts[1], sh, 1)
        rev = jnp.where(lane_w < GROUP_CH * (s + 1), rev, 0.0)
        skip = jnp.where(lane_w == GROUP_CH * s + row_w, dcol, 0.0)
        m_ref[s] = (fwd + rev + skip).astype(m_ref.dtype)


def _ssm_params(a_re, a_im, log_dt, b_re, b_im, c_re, c_im, d_skip):
    g = SSM_GROUPS
    dup = lambda a: jnp.concatenate([a, a], axis=-1)
    a_re2 = dup(jnp.transpose(a_re, (1, 0, 2)))
    a_im2 = dup(jnp.transpose(a_im, (1, 0, 2)))
    ldt = jnp.transpose(log_dt, (1, 0))[:, :, None]
    btr = jnp.transpose(b_re, (1, 0, 3, 2))
    bti = jnp.transpose(b_im, (1, 0, 3, 2))
    bt = jnp.concatenate([btr, bti], axis=-1)
    btsw = jnp.concatenate([bti, btr], axis=-1)
    cdr = dup(jnp.transpose(c_re, (1, 0, 2, 3)))
    cdi = dup(jnp.transpose(c_im, (1, 0, 2, 3)))
    dcol = d_skip.reshape(g, GROUP_CH, 1)
    g3 = lambda n: pl.BlockSpec((None, 2, n), lambda i: (i, 0, 0))
    g4 = lambda r, n: pl.BlockSpec((None, 2, r, n), lambda i: (i, 0, 0, 0))
    tc = SSM_CHUNK
    m, w, v, a = pl.pallas_call(
        _ssm_param_kernel,
        grid=(g,),
        in_specs=[g3(STATE_W), g3(STATE_W), g3(1),
                  g4(GROUP_CH, STATE_W), g4(GROUP_CH, STATE_W),
                  g4(GROUP_CH, STATE_W), g4(GROUP_CH, STATE_W),
                  pl.BlockSpec((None, GROUP_CH, 1), lambda i: (i, 0, 0))],
        out_specs=[pl.BlockSpec((None, tc, GROUP_CH, CHUNK_W), lambda i: (i, 0, 0, 0)),
                   pl.BlockSpec((None, 2, tc, GROUP_CH, STATE_W), lambda i: (i, 0, 0, 0, 0)),
                   pl.BlockSpec((None, 2, tc, GROUP_CH, STATE_W), lambda i: (i, 0, 0, 0, 0)),
                   pl.BlockSpec((None, 2, 2, STATE_W), lambda i: (i, 0, 0, 0))],
        out_shape=[jax.ShapeDtypeStruct((g, tc, GROUP_CH, CHUNK_W), BF16),
                   jax.ShapeDtypeStruct((g, 2, tc, GROUP_CH, STATE_W), BF16),
                   jax.ShapeDtypeStruct((g, 2, tc, GROUP_CH, STATE_W), BF16),
                   jax.ShapeDtypeStruct((g, 2, 2, STATE_W), F32)],
        compiler_params=_cparams("arbitrary"),
        name="ssm_params",
    )(a_re2, a_im2, ldt, bt, btsw, cdr, cdi, dcol)
    return (m.reshape(g, CHUNK_W, CHUNK_W), w.reshape(g, 2, CHUNK_W, STATE_W),
            v.reshape(g, 2, CHUNK_W, STATE_W), a)


def _ssm_kernel(bsz, zl_ref, zc_ref, m_ref, w_ref, v_ref, a_ref, y_ref, s_ref, h_ref):
    n_lat = zl_ref.shape[0]
    n_ctx = zc_ref.shape[0]
    zl = zl_ref[...]
    zc = zc_ref[...]
    for d in range(2):
        s_ref[d, 0:n_lat] = _dot(zl, w_ref[d])
        s_ref[d, n_lat:n_lat + n_ctx] = _dot(zc, w_ref[d])

    def step(d, h, s):
        return a_ref[d, 0:1, :] * h + a_ref[d, 1:2, :] * pltpu.roll(h, SSM_STATE, 1) + s

    def rows(j):
        return pl.ds(pl.multiple_of(j * bsz, bsz), bsz)

    nc_ctx = n_ctx // bsz
    nc_lat = n_lat // bsz

    def ctx_body(j, hs):
        hf, hr = hs
        hf = step(0, hf, s_ref[0, rows(nc_lat + j)])
        hr = step(1, hr, s_ref[1, rows(nc_lat + nc_ctx - 1 - j)])
        return hf, hr

    zero = jnp.zeros((bsz, STATE_W), F32)
    hs = lax.fori_loop(0, nc_ctx, ctx_body, (zero, zero))

    def lat_body(j, hs):
        hf, hr = hs
        jr = nc_lat - 1 - j
        h_ref[0, rows(j)] = hf
        h_ref[1, rows(jr)] = hr
        hf = step(0, hf, s_ref[0, rows(j)])
        hr = step(1, hr, s_ref[1, rows(jr)])
        return hf, hr

    lax.fori_loop(0, nc_lat, lat_body, hs)
    y = _dot(zl, m_ref[...])
    y += _dot_nt(h_ref[0].astype(BF16), v_ref[0])
    y += _dot_nt(h_ref[1].astype(BF16), v_ref[1])
    y_ref[...] = y.astype(y_ref.dtype)


def _ssm_scan(zl, zc, m, w, v, a, bsz):
    g, n_lat, _ = zl.shape
    n_ctx = zc.shape[1]
    blk = lambda *s: pl.BlockSpec((None,) + s, lambda i: (i,) + (0,) * len(s))
    return pl.pallas_call(
        functools.partial(_ssm_kernel, bsz),
        grid=(g,),
        in_specs=[blk(n_lat, CHUNK_W), blk(n_ctx, CHUNK_W), blk(CHUNK_W, CHUNK_W),
                  blk(2, CHUNK_W, STATE_W), blk(2, CHUNK_W, STATE_W), blk(2, 2, STATE_W)],
        out_specs=blk(n_lat, CHUNK_W),
        out_shape=jax.ShapeDtypeStruct((g, n_lat, CHUNK_W), BF16),
        scratch_shapes=[pltpu.VMEM((2, n_lat + n_ctx, STATE_W), F32),
                        pltpu.VMEM((2, n_lat, STATE_W), F32)],
        compiler_params=_cparams("arbitrary"),
        name="ssm_scan",
    )(zl, zc, m, w, v, a)


def _to_chunks(u2d, bsz, n_tok):
    nch = n_tok // SSM_CHUNK
    z = u2d.reshape(bsz, nch, SSM_CHUNK, SSM_GROUPS, GROUP_CH)
    z = jnp.transpose(z, (3, 1, 0, 2, 4))
    return z.reshape(SSM_GROUPS, nch * bsz, CHUNK_W)


def _from_chunks(y, bsz, n_tok):
    nch = n_tok // SSM_CHUNK
    z = y.reshape(SSM_GROUPS, nch, bsz, SSM_CHUNK, GROUP_CH)
    z = jnp.transpose(z, (2, 1, 3, 0, 4))
    return z.reshape(bsz * n_tok, D_SSM)


def _attn_kernel(lam_ref, g_ref, q_ref, kl_ref, vl_ref, kc_ref, vc_ref, o_ref):
    tq = q_ref.shape[0]
    lp = lam_ref[...]
    lam = (jnp.exp(jnp.sum(lp[0:1] * lp[1:2], axis=1, keepdims=True))
           - jnp.exp(jnp.sum(lp[2:3] * lp[3:4], axis=1, keepdims=True)) + LAM_INIT)
    q = q_ref[...]
    lane = lax.broadcasted_iota(jnp.int32, q.shape, 1)
    zero = jnp.zeros_like(q)
    qq = jnp.concatenate([jnp.where(lane < HEAD_DIM, q, zero),
                          jnp.where(lane >= HEAD_DIM, q, zero)], axis=0)
    sl = _dot_nt(qq, kl_ref[...])
    sc = _dot_nt(qq, kc_ref[...])
    mx = jnp.maximum(jnp.max(sl, axis=-1, keepdims=True), jnp.max(sc, axis=-1, keepdims=True))
    p_l = jnp.exp(sl - mx)
    p_c = jnp.exp(sc - mx)
    den = jnp.sum(p_l, axis=-1, keepdims=True) + jnp.sum(p_c, axis=-1, keepdims=True)
    o2 = _dot(p_l.astype(BF16), vl_ref[...]) + _dot(p_c.astype(BF16), vc_ref[...])
    o2 = o2 / den
    o = o2[0:tq] - lam * o2[tq:2 * tq]
    o = o * lax.rsqrt(jnp.mean(o * o, axis=-1, keepdims=True) + EPS) * g_ref[...] * (1.0 - LAM_INIT)
    o_ref[...] = o.astype(o_ref.dtype)


def _attention(lam_p, subln_g, q, k, v, kc, vc, bsz, n_lat, n_ctx):
    tq = min(256, n_lat)
    nq = n_lat // tq
    t = bsz * n_lat
    return pl.pallas_call(
        _attn_kernel,
        grid=(bsz, N_HEADS, nq),
        in_specs=[pl.BlockSpec((4, HEAD_DIM), lambda b, h, i: (0, 0)),
                  pl.BlockSpec((1, V_DIM), lambda b, h, i: (0, 0)),
                  pl.BlockSpec((None, tq, LANES), lambda b, h, i: (h, b * nq + i, 0)),
                  pl.BlockSpec((None, n_lat, LANES), lambda b, h, i: (h, b, 0)),
                  pl.BlockSpec((None, n_lat, LANES), lambda b, h, i: (h, b, 0)),
                  pl.BlockSpec((None, n_ctx, LANES), lambda b, h, i: (h, b, 0)),
                  pl.BlockSpec((None, n_ctx, LANES), lambda b, h, i: (h, b, 0))],
        out_specs=pl.BlockSpec((None, tq, LANES), lambda b, h, i: (h, b * nq + i, 0)),
        out_shape=jax.ShapeDtypeStruct((N_HEADS, t, LANES), BF16),
        compiler_params=_cparams("arbitrary", "arbitrary", "arbitrary"),
        name="diff_attn",
    )(lam_p, subln_g, q, k, v, kc, vc)


def _merge_kernel(y_ref, o_ref, gs_ref, ga_ref, x_ref, mod_ref, wg_ref, bg_ref, wps_ref, wpa_ref,
                  wo_ref, out_ref):
    y = y_ref[...].astype(F32)
    z = 0.5 * y * (1.0 + jnp.tanh(math.sqrt(2.0 / math.pi) * (y + 0.044715 * (y * y * y))))
    gl = _dot(z.astype(BF16), wg_ref[...]) + bg_ref[...]
    ys = (z * _sigmoid(gl)).astype(BF16)
    oa = jnp.concatenate([o_ref[h] for h in range(N_HEADS)], axis=1)
    merged = (gs_ref[...].astype(F32) * _dot(ys, wps_ref[...])
              + ga_ref[...].astype(F32) * _dot(oa, wpa_ref[...]))
    out = _dot(merged.astype(BF16), wo_ref[...])
    out_ref[...] = x_ref[...] + mod_ref[2:3, :] * out


def _merge(y, o, gs, ga, x2d, mod, w_glu, b_glu, w_ps, w_pa, w_o, n_lat):
    t = x2d.shape[0]
    tm = min(256, n_lat)
    tiles_per_b = n_lat // tm
    row = lambda i: (i, 0)
    const = lambda shape: pl.BlockSpec(shape, lambda i: (0,) * len(shape),
                                       pipeline_mode=pl.Buffered(1))
    return pl.pallas_call(
        _merge_kernel,
        grid=(t // tm,),
        in_specs=[pl.BlockSpec((tm, D_SSM), row),
                  pl.BlockSpec((N_HEADS, tm, LANES), lambda i: (0, i, 0)),
                  pl.BlockSpec((tm, D_MODEL), row),
                  pl.BlockSpec((tm, D_MODEL), row),
                  pl.BlockSpec((tm, D_MODEL), row),
                  pl.BlockSpec((None, 6, D_MODEL), lambda i: (i // tiles_per_b, 0, 0)),
                  const((D_SSM, D_SSM)), const((1, D_SSM)),
                  const((D_SSM, D_MODEL)), const((D_ATTN, D_MODEL)), const((D_MODEL, D_MODEL))],
        out_specs=pl.BlockSpec((tm, D_MODEL), row),
        out_shape=jax.ShapeDtypeStruct((t, D_MODEL), F32),
        compiler_params=_cparams("arbitrary"),
        name="merge",
    )(y, o, gs, ga, x2d, mod, w_glu, b_glu, w_ps, w_pa, w_o)


def _ffn_kernel(x_ref, mod_ref, g2_ref, gf_ref, wg_ref, wu_ref, wo_ref, out_ref, h_ref, acc_ref):
    j = pl.program_id(1)

    @pl.when(j == 0)
    def _():
        h_ref[...] = _norm_mod(x_ref[...], g2_ref[...], mod_ref[3:4, :], mod_ref[4:5, :]).astype(BF16)
        acc_ref[...] = jnp.zeros_like(acc_ref)

    h = h_ref[...]
    gate = _dot(h, wg_ref[...])
    up = _dot(h, wu_ref[...])
    act = (gate * _sigmoid(gate) * up).astype(BF16)
    acc_ref[...] += _dot(act, wo_ref[...])

    @pl.when(j == pl.num_programs(1) - 1)
    def _():
        xo = x_ref[...] + mod_ref[5:6, :] * acc_ref[...]
        ms = jnp.mean(xo * xo, axis=-1, keepdims=True)
        out_ref[...] = xo * lax.rsqrt(ms + EPS) * gf_ref[...]


def _ffn(x_mid, mod, norm2_g, norm_f_g, w_ffn_in, w_ffn_out, n_lat):
    t = x_mid.shape[0]
    tm = min(512, n_lat)
    tf = 512
    nf = D_FF // tf
    tiles_per_b = n_lat // tm
    row = lambda i, j: (i, 0)
    return pl.pallas_call(
        _ffn_kernel,
        grid=(t // tm, nf),
        in_specs=[pl.BlockSpec((tm, D_MODEL), row),
                  pl.BlockSpec((None, 6, D_MODEL), lambda i, j: (i // tiles_per_b, 0, 0)),
                  pl.BlockSpec((1, D_MODEL), lambda i, j: (0, 0)),
                  pl.BlockSpec((1, D_MODEL), lambda i, j: (0, 0)),
                  pl.BlockSpec((D_MODEL, tf), lambda i, j: (0, j)),
                  pl.BlockSpec((D_MODEL, tf), lambda i, j: (0, j + nf)),
                  pl.BlockSpec((tf, D_MODEL), lambda i, j: (j, 0))],
        out_specs=pl.BlockSpec((tm, D_MODEL), row),
        out_shape=jax.ShapeDtypeStruct((t, D_MODEL), F32),
        scratch_shapes=[pltpu.VMEM((tm, D_MODEL), BF16), pltpu.VMEM((tm, D_MODEL), F32)],
        compiler_params=_cparams("arbitrary", "arbitrary"),
        name="ffn",
    )(x_mid, mod, norm2_g, norm_f_g, w_ffn_in, w_ffn_in, w_ffn_out)


def kernel(x, c, ctx, c_ctx, w_ada, b_ada, norm1_g, w_in, ssm_a_re, ssm_a_im, ssm_log_dt, ssm_b_re, ssm_b_im, ssm_c_re, ssm_c_im, ssm_d, w_glu, b_glu, lambda_q1, lambda_k1, lambda_q2, lambda_k2, subln_g, w_proj_ssm, w_proj_attn, w_out, norm2_g, w_ffn_in, w_ffn_out, norm_f_g):
    bsz, n_lat, d = x.shape
    n_ctx = ctx.shape[1]
    assert d == D_MODEL and w_ada.shape[0] == 1, "single-layer block only"
    assert n_lat % GRID_W == 0 and n_lat % SSM_CHUNK == 0 and n_ctx % SSM_CHUNK == 0

    pad = (-(bsz + 1)) % 8
    cc = jnp.concatenate([c, c_ctx[None, :], jnp.zeros((pad, d), F32)], axis=0)
    mod_all = _adaln(cc, w_ada[0], b_ada[0])
    mod = mod_all[:bsz].reshape(bsz, 6, d)
    mod_c = mod_all[bsz:bsz + 1].reshape(1, 6, d)

    w_in_b = w_in[0].astype(BF16)
    g1 = norm1_g[0].reshape(1, d)
    x2d = x.reshape(bsz * n_lat, d)
    c2d = ctx.reshape(bsz * n_ctx, d)
    u, q, k, v, gs, ga = _inproj_lat(x2d, mod, g1, w_in_b, n_lat)
    uc, kc, vc = _inproj_ctx(c2d, mod_c, g1, w_in_b, n_ctx)

    m, w, vv, a = _ssm_params(ssm_a_re[0], ssm_a_im[0], ssm_log_dt[0], ssm_b_re[0], ssm_b_im[0],
                              ssm_c_re[0], ssm_c_im[0], ssm_d[0])
    y = _ssm_scan(_to_chunks(u, bsz, n_lat), _to_chunks(uc, bsz, n_ctx), m, w, vv, a, bsz)
    y = _from_chunks(y, bsz, n_lat)

    lam_p = jnp.stack([lambda_q1[0], lambda_k1[0], lambda_q2[0], lambda_k2[0]]).astype(F32)
    o = _attention(lam_p, subln_g[0].reshape(1, V_DIM), q, k, v, kc, vc, bsz, n_lat, n_ctx)

    x_mid = _merge(y, o, gs, ga, x2d, mod, w_glu[0].astype(BF16), b_glu[0].reshape(1, D_SSM),
                   w_proj_ssm[0].astype(BF16), w_proj_attn[0].astype(BF16), w_out[0].astype(BF16),
                   n_lat)
    out = _ffn(x_mid, mod, norm2_g[0].reshape(1, d), norm_f_g.reshape(1, d),
               w_ffn_in[0].astype(BF16), w_ffn_out[0].astype(BF16), n_lat)
    return out.reshape(bsz, n_lat, d)
```

```python
import functools
import math

import jax
import jax.numpy as jnp
from jax import lax
from jax.experimental import pallas as pl
from jax.experimental.pallas import tpu as pltpu

F32 = jnp.float32
BF16 = jnp.bfloat16

D_MODEL = 2048
GRID_W = 64
D_SSM = 1024
GROUP_CH = 16
SSM_GROUPS = D_SSM // GROUP_CH
SSM_STATE = 64
N_HEADS = 8
HEAD_DIM = 64
V_DIM = 2 * HEAD_DIM
D_QK = N_HEADS * 2 * HEAD_DIM
D_ATTN = N_HEADS * V_DIM
ATTN_SCALE = HEAD_DIM ** -0.5
Q_SCALE = ATTN_SCALE * math.log2(math.e)
ROPE_AXIS_DIM = HEAD_DIM // 2
ROPE_BASE = 10000.0
D_FF = 5632
D_IN = D_SSM + 2 * D_QK + D_ATTN + 2 * D_MODEL
EPS = 1e-6
LAM_INIT = 0.8 - 0.6 * math.exp(-0.3 * 0)

LANES = 128
SSM_CHUNK = 16
CHUNK_W = SSM_CHUNK * GROUP_CH
STATE_W = 2 * SSM_STATE
VMEM_LIMIT = 56 * 1024 * 1024


def _cparams(*sem):
    return pltpu.CompilerParams(dimension_semantics=sem, vmem_limit_bytes=VMEM_LIMIT)


def _dot(a, b):
    return jnp.dot(a, b, preferred_element_type=F32)


def _dot_nt(a, b, precision=None):
    return lax.dot_general(a, b, (((1,), (1,)), ((), ())), preferred_element_type=F32,
                           precision=precision)


def _sigmoid(x):
    return 1.0 / (1.0 + jnp.exp(-x))


def _adaln_kernel(c_ref, w_ref, b_ref, o_ref):
    c = c_ref[...]
    s = (c * _sigmoid(c)).astype(BF16)
    o_ref[...] = _dot(s, w_ref[...].astype(BF16)) + b_ref[...]


def _adaln(cc, w_ada, b_ada):
    rows, d = cc.shape
    n = w_ada.shape[1]
    tn = 1024
    return pl.pallas_call(
        _adaln_kernel,
        grid=(n // tn,),
        in_specs=[pl.BlockSpec((rows, d), lambda j: (0, 0)),
                  pl.BlockSpec((d, tn), lambda j: (0, j)),
                  pl.BlockSpec((1, tn), lambda j: (0, j))],
        out_specs=pl.BlockSpec((rows, tn), lambda j: (0, j)),
        out_shape=jax.ShapeDtypeStruct((rows, n), F32),
        compiler_params=_cparams("arbitrary"),
        name="adaln",
    )(cc, w_ada, b_ada.reshape(1, n))


def _norm_mod(x, g, shift, scale):
    ms = jnp.mean(x * x, axis=-1, keepdims=True)
    y = x * lax.rsqrt(ms + EPS) * g
    return y * (1.0 + scale) + shift


def _store_heads(ref, val):
    for h in range(N_HEADS):
        ref[h] = val[:, h * LANES:(h + 1) * LANES].astype(ref.dtype)


def _rope(acc, cos, sin_a, sin_b, scale):
    outs = []
    for h in range(acc.shape[1] // LANES):
        a = acc[:, h * LANES:(h + 1) * LANES]
        r = a * cos + pltpu.roll(a, LANES - 16, 1) * sin_a + pltpu.roll(a, 16, 1) * sin_b
        outs.append(r * scale if scale != 1.0 else r)
    return outs


def _inproj_lat_kernel(x_ref, mod_ref, g_ref, w_ref, cos_ref, sa_ref, sb_ref,
                       u_ref, q_ref, k_ref, v_ref, gs_ref, ga_ref, h_ref):
    j = pl.program_id(1)

    @pl.when(j == 0)
    def _():
        h_ref[...] = _norm_mod(x_ref[...], g_ref[...], mod_ref[0:1, :], mod_ref[1:2, :]).astype(BF16)

    acc = _dot(h_ref[...], w_ref[...])
    half = D_MODEL // 2

    @pl.when(j == 0)
    def _():
        u_ref[...] = acc.astype(BF16)

    @pl.when(j == 1)
    def _():
        for h, r in enumerate(_rope(acc, cos_ref[...], sa_ref[...], sb_ref[...], Q_SCALE)):
            q_ref[h] = r.astype(BF16)

    @pl.when(j == 2)
    def _():
        for h, r in enumerate(_rope(acc, cos_ref[...], sa_ref[...], sb_ref[...], 1.0)):
            k_ref[h] = r.astype(BF16)

    @pl.when(j == 3)
    def _():
        _store_heads(v_ref, acc)

    for jj, (ref, lo) in enumerate(((gs_ref, 0), (gs_ref, half), (ga_ref, 0), (ga_ref, half))):
        @pl.when(j == 4 + jj)
        def _(ref=ref, lo=lo):
            ref[:, lo:lo + half] = _sigmoid(acc).astype(BF16)


def _inproj_ctx_kernel(x_ref, mod_ref, g_ref, w_ref, u_ref, k_ref, v_ref, h_ref):
    j = pl.program_id(1)

    @pl.when(j == 0)
    def _():
        h_ref[...] = _norm_mod(x_ref[...], g_ref[...], mod_ref[0:1, :], mod_ref[1:2, :]).astype(BF16)

    acc = _dot(h_ref[...], w_ref[...])

    @pl.when(j == 0)
    def _():
        u_ref[...] = acc.astype(BF16)

    @pl.when(j == 1)
    def _():
        _store_heads(k_ref, acc)

    @pl.when(j == 2)
    def _():
        _store_heads(v_ref, acc)


def _rope_tables(n_lat):
    pos = jnp.arange(n_lat, dtype=jnp.int32)
    row = (pos // GRID_W).astype(F32)
    col = (pos % GRID_W).astype(F32)
    inv = 1.0 / (ROPE_BASE ** (jnp.arange(0, ROPE_AXIS_DIM, 2, dtype=F32) / ROPE_AXIS_DIM))
    lane = jnp.arange(LANES)
    d = lane % HEAD_DIM
    freq = inv[d % (ROPE_AXIS_DIM // 2)]
    ang = jnp.where(d < ROPE_AXIS_DIM, row[:, None], col[:, None]) * freq[None, :]
    first = (lane % ROPE_AXIS_DIM) < (ROPE_AXIS_DIM // 2)
    cos, sin = jnp.cos(ang), jnp.sin(ang)
    return cos, jnp.where(first, -sin, 0.0), jnp.where(first, 0.0, sin)


def _inproj_lat(x2d, mod, norm_g, w_in, n_lat):
    t = x2d.shape[0]
    tm = min(512, n_lat)
    tn = D_SSM
    tiles_per_b = n_lat // tm
    cos, sa, sb = _rope_tables(n_lat)
    row = lambda i, j: (i, 0)
    hm = lambda i, j: (0, i, 0)
    tab = pl.BlockSpec((tm, LANES), lambda i, j: (i % tiles_per_b, 0))
    return pl.pallas_call(
        _inproj_lat_kernel,
        grid=(t // tm, D_IN // tn),
        in_specs=[pl.BlockSpec((tm, D_MODEL), row),
                  pl.BlockSpec((None, 6, D_MODEL), lambda i, j: (i // tiles_per_b, 0, 0)),
                  pl.BlockSpec((1, D_MODEL), lambda i, j: (0, 0)),
                  pl.BlockSpec((D_MODEL, tn), lambda i, j: (0, j)),
                  tab, tab, tab],
        out_specs=[pl.BlockSpec((tm, D_SSM), row),
                   pl.BlockSpec((N_HEADS, tm, LANES), hm),
                   pl.BlockSpec((N_HEADS, tm, LANES), hm),
                   pl.BlockSpec((N_HEADS, tm, LANES), hm),
                   pl.BlockSpec((tm, D_MODEL), row),
                   pl.BlockSpec((tm, D_MODEL), row)],
        out_shape=[jax.ShapeDtypeStruct((t, D_SSM), BF16),
                   jax.ShapeDtypeStruct((N_HEADS, t, LANES), BF16),
                   jax.ShapeDtypeStruct((N_HEADS, t, LANES), BF16),
                   jax.ShapeDtypeStruct((N_HEADS, t, LANES), BF16),
                   jax.ShapeDtypeStruct((t, D_MODEL), BF16),
                   jax.ShapeDtypeStruct((t, D_MODEL), BF16)],
        scratch_shapes=[pltpu.VMEM((tm, D_MODEL), BF16)],
        compiler_params=_cparams("arbitrary", "arbitrary"),
        name="inproj_lat",
    )(x2d, mod, norm_g, w_in, cos, sa, sb)


def _inproj_ctx(c2d, mod_c, norm_g, w_in, n_ctx):
    t = c2d.shape[0]
    tm = min(512, t)
    tn = D_SSM
    row = lambda i, j: (i, 0)
    hm = lambda i, j: (0, i, 0)
    return pl.pallas_call(
        _inproj_ctx_kernel,
        grid=(t // tm, 3),
        in_specs=[pl.BlockSpec((tm, D_MODEL), row),
                  pl.BlockSpec((None, 6, D_MODEL), lambda i, j: (0, 0, 0)),
                  pl.BlockSpec((1, D_MODEL), lambda i, j: (0, 0)),
                  pl.BlockSpec((D_MODEL, tn), lambda i, j: (0, j + jnp.minimum(j, 1)))],
        out_specs=[pl.BlockSpec((tm, D_SSM), row),
                   pl.BlockSpec((N_HEADS, tm, LANES), hm),
                   pl.BlockSpec((N_HEADS, tm, LANES), hm)],
        out_shape=[jax.ShapeDtypeStruct((t, D_SSM), BF16),
                   jax.ShapeDtypeStruct((N_HEADS, t, LANES), BF16),
                   jax.ShapeDtypeStruct((N_HEADS, t, LANES), BF16)],
        scratch_shapes=[pltpu.VMEM((tm, D_MODEL), BF16)],
        compiler_params=_cparams("arbitrary", "arbitrary"),
        name="inproj_ctx",
    )(c2d, mod_c, norm_g, w_in)


def _ssm_param_kernel(a_re_ref, a_im_ref, ldt_ref, bt_ref, btsw_ref, cr_ref, ci_ref, d_ref,
                      m_ref, w_ref, v_ref, a_ref):
    tc = SSM_CHUNK
    lane = lax.broadcasted_iota(jnp.int32, (1, STATE_W), 1)
    lo = lane < SSM_STATE
    kts = []
    for d in range(2):
        dt = jnp.exp(ldt_ref[d:d + 1, :])
        lre = a_re_ref[d:d + 1, :] * dt
        lim = a_im_ref[d:d + 1, :] * dt
        kk = lax.broadcasted_iota(jnp.int32, (tc + 1, STATE_W), 0).astype(F32)
        mag = jnp.exp(kk * lre)
        c2 = mag * jnp.cos(kk * lim)
        s2 = mag * jnp.sin(kk * lim)
        xr = c2[1:2] - 1.0
        xi = s2[1:2]
        ar = a_re_ref[d:d + 1, :]
        ai = a_im_ref[d:d + 1, :]
        den = ar * ar + ai * ai
        er = (xr * ar + xi * ai) / den
        ei = (xi * ar - xr * ai) / den
        e2 = jnp.where(lo, -ei, ei)
        bt = bt_ref[d]
        btsw = btsw_ref[d]
        bb = er * bt + e2 * btsw
        bbsw = er * btsw - e2 * bt
        cdr = cr_ref[d]
        cdi = ci_ref[d]
        p2 = jnp.where(lo, -s2, s2)
        q1 = jnp.where(lo, c2, -s2)
        q2 = jnp.where(lo, -s2, -c2)
        r1 = jnp.where(lo, c2, s2)
        r2 = jnp.where(lo, -s2, c2)
        cl = []
        for s in range(tc):
            kw = tc - 1 - s if d == 0 else s
            w_ref[d, s] = (c2[kw:kw + 1] * bb + p2[kw:kw + 1] * bbsw).astype(w_ref.dtype)
            kv = s + 1 if d == 0 else tc - s
            v_ref[d, s] = (cdr * q1[kv:kv + 1] + cdi * q2[kv:kv + 1]).astype(v_ref.dtype)
            kl = s if d == 0 else tc - 1 - s
            cl.append(cdr * r1[kl:kl + 1] + cdi * r2[kl:kl + 1])
        clcat = jnp.concatenate(cl, axis=0)
        bneg = jnp.where(lo, bb, -bb)
        kts.append(_dot_nt(bneg, clcat, precision=lax.Precision.HIGHEST))
        a_ref[d, 0:1, :] = c2[tc:tc + 1]
        a_ref[d, 1:2, :] = p2[tc:tc + 1]
    lane_w = lax.broadcasted_iota(jnp.int32, (GROUP_CH, CHUNK_W), 1)
    row_w = lax.broadcasted_iota(jnp.int32, (GROUP_CH, CHUNK_W), 0)
    dcol = d_ref[...]
    for s in range(tc):
        fwd = kts[0] if s == 0 else pltpu.roll(kts[0], GROUP_CH * s, 1)
        fwd = jnp.where(lane_w >= GROUP_CH * s, fwd, 0.0)
        sh = (CHUNK_W - GROUP_CH * (tc - 1 - s)) % CHUNK_W
        rev = kts[1] if sh == 0 else pltpu.roll(kts[1], sh, 1)
        rev = jnp.where(lane_w < GROUP_CH * (s + 1), rev, 0.0)
        skip = jnp.where(lane_w == GROUP_CH * s + row_w, dcol, 0.0)
        m_ref[s] = (fwd + rev + skip).astype(m_ref.dtype)


def _ssm_params(a_re, a_im, log_dt, b_re, b_im, c_re, c_im, d_skip):
    g = SSM_GROUPS
    dup = lambda a: jnp.concatenate([a, a], axis=-1)
    a_re2 = dup(jnp.transpose(a_re, (1, 0, 2)))
    a_im2 = dup(jnp.transpose(a_im, (1, 0, 2)))
    ldt = jnp.transpose(log_dt, (1, 0))[:, :, None]
    btr = jnp.transpose(b_re, (1, 0, 3, 2))
    bti = jnp.transpose(b_im, (1, 0, 3, 2))
    bt = jnp.concatenate([btr, bti], axis=-1)
    btsw = jnp.concatenate([bti, btr], axis=-1)
    cdr = dup(jnp.transpose(c_re, (1, 0, 2, 3)))
    cdi = dup(jnp.transpose(c_im, (1, 0, 2, 3)))
    dcol = d_skip.reshape(g, GROUP_CH, 1)
    g3 = lambda n: pl.BlockSpec((None, 2, n), lambda i: (i, 0, 0))
    g4 = lambda r, n: pl.BlockSpec((None, 2, r, n), lambda i: (i, 0, 0, 0))
    tc = SSM_CHUNK
    m, w, v, a = pl.pallas_call(
        _ssm_param_kernel,
        grid=(g,),
        in_specs=[g3(STATE_W), g3(STATE_W), g3(1),
                  g4(GROUP_CH, STATE_W), g4(GROUP_CH, STATE_W),
                  g4(GROUP_CH, STATE_W), g4(GROUP_CH, STATE_W),
                  pl.BlockSpec((None, GROUP_CH, 1), lambda i: (i, 0, 0))],
        out_specs=[pl.BlockSpec((None, tc, GROUP_CH, CHUNK_W), lambda i: (i, 0, 0, 0)),
                   pl.BlockSpec((None, 2, tc, GROUP_CH, STATE_W), lambda i: (i, 0, 0, 0, 0)),
                   pl.BlockSpec((None, 2, tc, GROUP_CH, STATE_W), lambda i: (i, 0, 0, 0, 0)),
                   pl.BlockSpec((None, 2, 2, STATE_W), lambda i: (i, 0, 0, 0))],
        out_shape=[jax.ShapeDtypeStruct((g, tc, GROUP_CH, CHUNK_W), BF16),
                   jax.ShapeDtypeStruct((g, 2, tc, GROUP_CH, STATE_W), BF16),
                   jax.ShapeDtypeStruct((g, 2, tc, GROUP_CH, STATE_W), BF16),
                   jax.ShapeDtypeStruct((g, 2, 2, STATE_W), F32)],
        compiler_params=_cparams("arbitrary"),
        name="ssm_params",
    )(a_re2, a_im2, ldt, bt, btsw, cdr, cdi, dcol)
    return (m.reshape(g, CHUNK_W, CHUNK_W), w.reshape(g, 2, CHUNK_W, STATE_W),
            v.reshape(g, 2, CHUNK_W, STATE_W), a)


def _ssm_kernel(bsz, zl_ref, zc_ref, m_ref, w_ref, v_ref, a_ref, y_ref, s_ref, h_ref):
    n_lat = zl_ref.shape[0]
    n_ctx = zc_ref.shape[0]
    zl = zl_ref[...]
    zc = zc_ref[...]
    for d in range(2):
        s_ref[d, 0:n_lat] = _dot(zl, w_ref[d])
        s_ref[d, n_lat:n_lat + n_ctx] = _dot(zc, w_ref[d])

    def step(d, h, s):
        return a_ref[d, 0:1, :] * h + a_ref[d, 1:2, :] * pltpu.roll(h, SSM_STATE, 1) + s

    def rows(j):
        return pl.ds(pl.multiple_of(j * bsz, bsz), bsz)

    nc_ctx = n_ctx // bsz
    nc_lat = n_lat // bsz

    def ctx_body(j, hs):
        hf, hr = hs
        hf = step(0, hf, s_ref[0, rows(nc_lat + j)])
        hr = step(1, hr, s_ref[1, rows(nc_lat + nc_ctx - 1 - j)])
        return hf, hr

    zero = jnp.zeros((bsz, STATE_W), F32)
    hs = lax.fori_loop(0, nc_ctx, ctx_body, (zero, zero))

    def lat_body(j, hs):
        hf, hr = hs
        jr = nc_lat - 1 - j
        h_ref[0, rows(j)] = hf
        h_ref[1, rows(jr)] = hr
        hf = step(0, hf, s_ref[0, rows(j)])
        hr = step(1, hr, s_ref[1, rows(jr)])
        return hf, hr

    lax.fori_loop(0, nc_lat, lat_body, hs)
    y = _dot(zl, m_ref[...])
    y += _dot_nt(h_ref[0].astype(BF16), v_ref[0])
    y += _dot_nt(h_ref[1].astype(BF16), v_ref[1])
    y_ref[...] = y.astype(y_ref.dtype)


def _ssm_scan(zl, zc, m, w, v, a, bsz):
    g, n_lat, _ = zl.shape
    n_ctx = zc.shape[1]
    blk = lambda *s: pl.BlockSpec((None,) + s, lambda i: (i,) + (0,) * len(s))
    return pl.pallas_call(
        functools.partial(_ssm_kernel, bsz),
        grid=(g,),
        in_specs=[blk(n_lat, CHUNK_W), blk(n_ctx, CHUNK_W), blk(CHUNK_W, CHUNK_W),
                  blk(2, CHUNK_W, STATE_W), blk(2, CHUNK_W, STATE_W), blk(2, 2, STATE_W)],
        out_specs=blk(n_lat, CHUNK_W),
        out_shape=jax.ShapeDtypeStruct((g, n_lat, CHUNK_W), BF16),
        scratch_shapes=[pltpu.VMEM((2, n_lat + n_ctx, STATE_W), F32),
                        pltpu.VMEM((2, n_lat, STATE_W), F32)],
        compiler_params=_cparams("arbitrary"),
        name="ssm_scan",
    )(zl, zc, m, w, v, a)


def _to_chunks(u2d, bsz, n_tok):
    nch = n_tok // SSM_CHUNK
    z = u2d.reshape(bsz, nch, SSM_CHUNK, SSM_GROUPS, GROUP_CH)
    z = jnp.transpose(z, (3, 1, 0, 2, 4))
    return z.reshape(SSM_GROUPS, nch * bsz, CHUNK_W)


def _from_chunks(y, bsz, n_tok):
    nch = n_tok // SSM_CHUNK
    z = y.reshape(SSM_GROUPS, nch, bsz, SSM_CHUNK, GROUP_CH)
    z = jnp.transpose(z, (2, 1, 3, 0, 4))
    return z.reshape(bsz * n_tok, D_SSM)


ATTN_SUB = 64


def _attn_kernel(lam_ref, g_ref, q_ref, kl_ref, vl_ref, kc_ref, vc_ref, o_ref, k_scr, v_scr):
    tq = q_ref.shape[0]
    n_lat = kl_ref.shape[0]
    n_ctx = kc_ref.shape[0]

    @pl.when(pl.program_id(2) == 0)
    def _():
        k_scr[0:n_lat] = kl_ref[...]
        k_scr[n_lat:n_lat + n_ctx] = kc_ref[...]
        v_scr[0:n_lat, 0:V_DIM] = vl_ref[...]
        v_scr[n_lat:n_lat + n_ctx, 0:V_DIM] = vc_ref[...]
        v_scr[:, V_DIM:2 * V_DIM] = jnp.ones((n_lat + n_ctx, V_DIM), BF16)

    lp = lam_ref[...]
    lam = (jnp.exp(jnp.sum(lp[0:1] * lp[1:2], axis=1, keepdims=True))
           - jnp.exp(jnp.sum(lp[2:3] * lp[3:4], axis=1, keepdims=True)) + LAM_INIT)
    gain = g_ref[...] * (1.0 - LAM_INIT)
    lane = lax.broadcasted_iota(jnp.int32, (ATTN_SUB, LANES), 1)
    zero = jnp.zeros((ATTN_SUB, LANES), BF16)
    for c in range(tq // ATTN_SUB):
        q = q_ref[c * ATTN_SUB:(c + 1) * ATTN_SUB, :]
        qq = jnp.concatenate([jnp.where(lane < HEAD_DIM, q, zero),
                              jnp.where(lane >= HEAD_DIM, q, zero)], axis=0)
        s = _dot_nt(qq, k_scr[...])
        mx = jnp.max(s, axis=-1, keepdims=True)
        p = jnp.exp2((s - mx).astype(BF16))
        oe = _dot(p, v_scr[...])
        o2 = oe[:, 0:V_DIM] / oe[:, V_DIM:2 * V_DIM]
        o = o2[0:ATTN_SUB] - lam * o2[ATTN_SUB:2 * ATTN_SUB]
        o = o * lax.rsqrt(jnp.mean(o * o, axis=-1, keepdims=True) + EPS) * gain
        o_ref[c * ATTN_SUB:(c + 1) * ATTN_SUB, :] = o.astype(o_ref.dtype)


def _attention(lam_p, subln_g, q, k, v, kc, vc, bsz, n_lat, n_ctx):
    tq = min(512, n_lat)
    nq = n_lat // tq
    t = bsz * n_lat
    return pl.pallas_call(
        _attn_kernel,
        grid=(bsz, N_HEADS, nq),
        scratch_shapes=[pltpu.VMEM((n_lat + n_ctx, LANES), BF16),
                        pltpu.VMEM((n_lat + n_ctx, 2 * V_DIM), BF16)],
        in_specs=[pl.BlockSpec((4, HEAD_DIM), lambda b, h, i: (0, 0)),
                  pl.BlockSpec((1, V_DIM), lambda b, h, i: (0, 0)),
                  pl.BlockSpec((None, tq, LANES), lambda b, h, i: (h, b * nq + i, 0)),
                  pl.BlockSpec((None, n_lat, LANES), lambda b, h, i: (h, b, 0)),
                  pl.BlockSpec((None, n_lat, LANES), lambda b, h, i: (h, b, 0)),
                  pl.BlockSpec((None, n_ctx, LANES), lambda b, h, i: (h, b, 0)),
                  pl.BlockSpec((None, n_ctx, LANES), lambda b, h, i: (h, b, 0))],
        out_specs=pl.BlockSpec((None, tq, LANES), lambda b, h, i: (h, b * nq + i, 0)),
        out_shape=jax.ShapeDtypeStruct((N_HEADS, t, LANES), BF16),
        compiler_params=_cparams("arbitrary", "arbitrary", "arbitrary"),
        name="diff_attn",
    )(lam_p, subln_g, q, k, v, kc, vc)


def _merge_kernel(y_ref, o_ref, gs_ref, ga_ref, x_ref, mod_ref, wg_ref, bg_ref, wps_ref, wpa_ref,
                  wo_ref, out_ref):
    y = y_ref[...].astype(F32)
    z = 0.5 * y * (1.0 + jnp.tanh(math.sqrt(2.0 / math.pi) * (y + 0.044715 * (y * y * y))))
    gl = _dot(z.astype(BF16), wg_ref[...]) + bg_ref[...]
    ys = (z * _sigmoid(gl)).astype(BF16)
    oa = jnp.concatenate([o_ref[h] for h in range(N_HEADS)], axis=1)
    merged = (gs_ref[...].astype(F32) * _dot(ys, wps_ref[...])
              + ga_ref[...].astype(F32) * _dot(oa, wpa_ref[...]))
    out = _dot(merged.astype(BF16), wo_ref[...])
    out_ref[...] = x_ref[...] + mod_ref[2:3, :] * out


def _merge(y, o, gs, ga, x2d, mod, w_glu, b_glu, w_ps, w_pa, w_o, n_lat):
    t = x2d.shape[0]
    tm = min(256, n_lat)
    tiles_per_b = n_lat // tm
    row = lambda i: (i, 0)
    const = lambda shape: pl.BlockSpec(shape, lambda i: (0,) * len(shape),
                                       pipeline_mode=pl.Buffered(1))
    return pl.pallas_call(
        _merge_kernel,
        grid=(t // tm,),
        in_specs=[pl.BlockSpec((tm, D_SSM), row),
                  pl.BlockSpec((N_HEADS, tm, LANES), lambda i: (0, i, 0)),
                  pl.BlockSpec((tm, D_MODEL), row),
                  pl.BlockSpec((tm, D_MODEL), row),
                  pl.BlockSpec((tm, D_MODEL), row),
                  pl.BlockSpec((None, 6, D_MODEL), lambda i: (i // tiles_per_b, 0, 0)),
                  const((D_SSM, D_SSM)), const((1, D_SSM)),
                  const((D_SSM, D_MODEL)), const((D_ATTN, D_MODEL)), const((D_MODEL, D_MODEL))],
        out_specs=pl.BlockSpec((tm, D_MODEL), row),
        out_shape=jax.ShapeDtypeStruct((t, D_MODEL), F32),
        compiler_params=_cparams("arbitrary"),
        name="merge",
    )(y, o, gs, ga, x2d, mod, w_glu, b_glu, w_ps, w_pa, w_o)


def _ffn_kernel(x_ref, mod_ref, g2_ref, gf_ref, wg_ref, wu_ref, wo_ref, out_ref, h_ref, acc_ref):
    j = pl.program_id(1)

    @pl.when(j == 0)
    def _():
        h_ref[...] = _norm_mod(x_ref[...], g2_ref[...], mod_ref[3:4, :], mod_ref[4:5, :]).astype(BF16)
        acc_ref[...] = jnp.zeros_like(acc_ref)

    h = h_ref[...]
    gate = _dot(h, wg_ref[...])
    up = _dot(h, wu_ref[...])
    act = (gate * _sigmoid(gate) * up).astype(BF16)
    acc_ref[...] += _dot(act, wo_ref[...])

    @pl.when(j == pl.num_programs(1) - 1)
    def _():
        xo = x_ref[...] + mod_ref[5:6, :] * acc_ref[...]
        ms = jnp.mean(xo * xo, axis=-1, keepdims=True)
        out_ref[...] = xo * lax.rsqrt(ms + EPS) * gf_ref[...]


def _ffn(x_mid, mod, norm2_g, norm_f_g, w_ffn_in, w_ffn_out, n_lat):
    t = x_mid.shape[0]
    tm = min(512, n_lat)
    tf = 512
    nf = D_FF // tf
    tiles_per_b = n_lat // tm
    row = lambda i, j: (i, 0)
    return pl.pallas_call(
        _ffn_kernel,
        grid=(t // tm, nf),
        in_specs=[pl.BlockSpec((tm, D_MODEL), row),
                  pl.BlockSpec((None, 6, D_MODEL), lambda i, j: (i // tiles_per_b, 0, 0)),
                  pl.BlockSpec((1, D_MODEL), lambda i, j: (0, 0)),
                  pl.BlockSpec((1, D_MODEL), lambda i, j: (0, 0)),
                  pl.BlockSpec((D_MODEL, tf), lambda i, j: (0, j)),
                  pl.BlockSpec((D_MODEL, tf), lambda i, j: (0, j + nf)),
                  pl.BlockSpec((tf, D_MODEL), lambda i, j: (j, 0))],
        out_specs=pl.BlockSpec((tm, D_MODEL), row),
        out_shape=jax.ShapeDtypeStruct((t, D_MODEL), F32),
        scratch_shapes=[pltpu.VMEM((tm, D_MODEL), BF16), pltpu.VMEM((tm, D_MODEL), F32)],
        compiler_params=_cparams("arbitrary", "arbitrary"),
        name="ffn",
    )(x_mid, mod, norm2_g, norm_f_g, w_ffn_in, w_ffn_in, w_ffn_out)


def kernel(x, c, ctx, c_ctx, w_ada, b_ada, norm1_g, w_in, ssm_a_re, ssm_a_im, ssm_log_dt, ssm_b_re, ssm_b_im, ssm_c_re, ssm_c_im, ssm_d, w_glu, b_glu, lambda_q1, lambda_k1, lambda_q2, lambda_k2, subln_g, w_proj_ssm, w_proj_attn, w_out, norm2_g, w_ffn_in, w_ffn_out, norm_f_g):
    bsz, n_lat, d = x.shape
    n_ctx = ctx.shape[1]
    assert d == D_MODEL and w_ada.shape[0] == 1, "single-layer block only"
    assert n_lat % GRID_W == 0 and n_lat % SSM_CHUNK == 0 and n_ctx % SSM_CHUNK == 0

    pad = (-(bsz + 1)) % 8
    cc = jnp.concatenate([c, c_ctx[None, :], jnp.zeros((pad, d), F32)], axis=0)
    mod_all = _adaln(cc, w_ada[0], b_ada[0])
    mod = mod_all[:bsz].reshape(bsz, 6, d)
    mod_c = mod_all[bsz:bsz + 1].reshape(1, 6, d)

    w_in_b = w_in[0].astype(BF16)
    g1 = norm1_g[0].reshape(1, d)
    x2d = x.reshape(bsz * n_lat, d)
    c2d = ctx.reshape(bsz * n_ctx, d)
    u, q, k, v, gs, ga = _inproj_lat(x2d, mod, g1, w_in_b, n_lat)
    uc, kc, vc = _inproj_ctx(c2d, mod_c, g1, w_in_b, n_ctx)

    m, w, vv, a = _ssm_params(ssm_a_re[0], ssm_a_im[0], ssm_log_dt[0], ssm_b_re[0], ssm_b_im[0],
                              ssm_c_re[0], ssm_c_im[0], ssm_d[0])
    y = _ssm_scan(_to_chunks(u, bsz, n_lat), _to_chunks(uc, bsz, n_ctx), m, w, vv, a, bsz)
    y = _from_chunks(y, bsz, n_lat)

    lam_p = jnp.stack([lambda_q1[0], lambda_k1[0], lambda_q2[0], lambda_k2[0]]).astype(F32)
    o = _attention(lam_p, subln_g[0].reshape(1, V_DIM), q, k, v, kc, vc, bsz, n_lat, n_ctx)

    x_mid = _merge(y, o, gs, ga, x2d, mod, w_glu[0].astype(BF16), b_glu[0].reshape(1, D_SSM),
                   w_proj_ssm[0].astype(BF16), w_proj_attn[0].astype(BF16), w_out[0].astype(BF16),
                   n_lat)
    out = _ffn(x_mid, mod, norm2_g[0].reshape(1, d), norm_f_g.reshape(1, d),
               w_ffn_in[0].astype(BF16), w_ffn_out[0].astype(BF16), n_lat)
    return out.reshape(bsz, n_lat, d)
```

```python
import functools
import math

import jax
import jax.numpy as jnp
from jax import lax
from jax.experimental import pallas as pl
from jax.experimental.pallas import tpu as pltpu

F32 = jnp.float32
BF16 = jnp.bfloat16

D_MODEL = 2048
GRID_W = 64
D_SSM = 1024
GROUP_CH = 16
SSM_GROUPS = D_SSM // GROUP_CH
SSM_STATE = 64
N_HEADS = 8
HEAD_DIM = 64
V_DIM = 2 * HEAD_DIM
D_QK = N_HEADS * 2 * HEAD_DIM
D_ATTN = N_HEADS * V_DIM
ATTN_SCALE = HEAD_DIM ** -0.5
Q_SCALE = ATTN_SCALE * math.log2(math.e)
ROPE_AXIS_DIM = HEAD_DIM // 2
ROPE_BASE = 10000.0
D_FF = 5632
D_IN = D_SSM + 2 * D_QK + D_ATTN + 2 * D_MODEL
EPS = 1e-6
LAM_INIT = 0.8 - 0.6 * math.exp(-0.3 * 0)

LANES = 128
SUBLANES = 8
MXU_DIM = 256
SSM_CHUNK = 16
CHUNK_W = SSM_CHUNK * GROUP_CH
STATE_W = 2 * SSM_STATE
VMEM_LIMIT = 56 * 1024 * 1024


def _cparams(*sem):
    return pltpu.CompilerParams(dimension_semantics=sem, vmem_limit_bytes=VMEM_LIMIT)


def _dot(a, b):
    return jnp.dot(a, b, preferred_element_type=F32)


def _dot_nt(a, b, precision=None):
    return lax.dot_general(a, b, (((1,), (1,)), ((), ())), preferred_element_type=F32,
                           precision=precision)


def _dot_tn(a, b):
    return lax.dot_general(a, b, (((0,), (0,)), ((), ())), preferred_element_type=F32)


def _sigmoid(x):
    return 1.0 / (1.0 + jnp.exp(-x))


def _adaln_kernel(c_ref, w_ref, b_ref, o_ref):
    c = c_ref[...]
    s = (c * _sigmoid(c)).astype(BF16)
    o_ref[...] = _dot(s, w_ref[...].astype(BF16)) + b_ref[...]


def _adaln(cc, w_ada, b_ada):
    rows, d = cc.shape
    n = w_ada.shape[1]
    tn = 1024
    return pl.pallas_call(
        _adaln_kernel,
        grid=(n // tn,),
        in_specs=[pl.BlockSpec((rows, d), lambda j: (0, 0)),
                  pl.BlockSpec((d, tn), lambda j: (0, j)),
                  pl.BlockSpec((1, tn), lambda j: (0, j))],
        out_specs=pl.BlockSpec((rows, tn), lambda j: (0, j)),
        out_shape=jax.ShapeDtypeStruct((rows, n), F32),
        compiler_params=_cparams("arbitrary"),
        name="adaln",
    )(cc, w_ada, b_ada.reshape(1, n))


def _norm_mod(x, g, shift, scale):
    ms = jnp.mean(x * x, axis=-1, keepdims=True)
    y = x * lax.rsqrt(ms + EPS) * g
    return y * (1.0 + scale) + shift


def _prenorm_kernel(x_ref, mod_ref, g_ref, h_ref):
    h_ref[...] = _norm_mod(x_ref[...], g_ref[...], mod_ref[0:1, :], mod_ref[1:2, :]).astype(BF16)


def _prenorm(x2d, mod, norm_g, n_tok):
    t = x2d.shape[0]
    tm = min(512, n_tok)
    tiles_per_b = n_tok // tm
    per_batch = mod.shape[0] > 1
    return pl.pallas_call(
        _prenorm_kernel,
        grid=(t // tm,),
        in_specs=[pl.BlockSpec((tm, D_MODEL), lambda i: (i, 0)),
                  pl.BlockSpec((None, 6, D_MODEL),
                               (lambda i: (i // tiles_per_b, 0, 0)) if per_batch else (lambda i: (0, 0, 0))),
                  pl.BlockSpec((1, D_MODEL), lambda i: (0, 0))],
        out_specs=pl.BlockSpec((tm, D_MODEL), lambda i: (i, 0)),
        out_shape=jax.ShapeDtypeStruct((t, D_MODEL), BF16),
        compiler_params=_cparams("arbitrary"),
        name="prenorm",
    )(x2d, mod, norm_g)


PROJ_TN = 1024


def _rope_tables(n_lat):
    pos = jnp.arange(n_lat, dtype=jnp.int32)
    row = (pos // GRID_W).astype(F32)
    col = (pos % GRID_W).astype(F32)
    inv = 1.0 / (ROPE_BASE ** (jnp.arange(0, ROPE_AXIS_DIM, 2, dtype=F32) / ROPE_AXIS_DIM))
    lane = jnp.arange(LANES)
    d = lane % HEAD_DIM
    freq = inv[d % (ROPE_AXIS_DIM // 2)]
    ang = jnp.where(d < ROPE_AXIS_DIM, row[:, None], col[:, None]) * freq[None, :]
    first = (lane % ROPE_AXIS_DIM) < (ROPE_AXIS_DIM // 2)
    cos, sin = jnp.cos(ang), jnp.sin(ang)
    return cos, jnp.where(first, -sin, 0.0), jnp.where(first, 0.0, sin)


def _proj_rope_kernel(h_ref, w_ref, cos_ref, sa_ref, sb_ref, o_ref):
    scale = jnp.where(pl.program_id(1) == 0, Q_SCALE, 1.0).astype(F32)
    cos = cos_ref[...] * scale
    sa = sa_ref[...] * scale
    sb = sb_ref[...] * scale
    for c in range(PROJ_TN // MXU_DIM):
        acc = _dot(h_ref[...], w_ref[:, c * MXU_DIM:(c + 1) * MXU_DIM])
        for hh in range(MXU_DIM // LANES):
            a = acc[:, hh * LANES:(hh + 1) * LANES]
            r = a * cos + pltpu.roll(a, LANES - 16, 1) * sa + pltpu.roll(a, 16, 1) * sb
            o_ref[c * (MXU_DIM // LANES) + hh] = r.astype(BF16)


def _proj_heads_kernel(h_ref, w_ref, o_ref):
    for c in range(PROJ_TN // MXU_DIM):
        acc = _dot(h_ref[...], w_ref[:, c * MXU_DIM:(c + 1) * MXU_DIM])
        for hh in range(MXU_DIM // LANES):
            o_ref[c * (MXU_DIM // LANES) + hh] = acc[:, hh * LANES:(hh + 1) * LANES].astype(BF16)


def _proj_gate_kernel(h_ref, w_ref, o_ref):
    for c in range(PROJ_TN // MXU_DIM):
        acc = _dot(h_ref[...], w_ref[:, c * MXU_DIM:(c + 1) * MXU_DIM])
        o_ref[:, c * MXU_DIM:(c + 1) * MXU_DIM] = _sigmoid(acc).astype(BF16)


def _proj_tm(t):
    return min(1024, t)


def _proj_rope(h, w_in, n_lat):
    t = h.shape[0]
    tm = min(_proj_tm(t), n_lat)
    tiles_per_b = n_lat // tm
    tab = pl.BlockSpec((tm, LANES), lambda i, j: (i % tiles_per_b, 0))
    col0 = D_SSM // PROJ_TN
    return pl.pallas_call(
        _proj_rope_kernel,
        grid=(t // tm, 2),
        in_specs=[pl.BlockSpec((tm, D_MODEL), lambda i, j: (i, 0)),
                  pl.BlockSpec((D_MODEL, PROJ_TN), lambda i, j: (0, col0 + j)),
                  tab, tab, tab],
        out_specs=pl.BlockSpec((None, N_HEADS, tm, LANES), lambda i, j: (j, 0, i, 0)),
        out_shape=jax.ShapeDtypeStruct((2, N_HEADS, t, LANES), BF16),
        compiler_params=_cparams("arbitrary", "arbitrary"),
        name="proj_qk",
    )(h, w_in, *_rope_tables(n_lat))


def _proj_heads(h, w_in, col0, n_out):
    t = h.shape[0]
    tm = _proj_tm(t)
    return pl.pallas_call(
        _proj_heads_kernel,
        grid=(t // tm, n_out),
        in_specs=[pl.BlockSpec((tm, D_MODEL), lambda i, j: (i, 0)),
                  pl.BlockSpec((D_MODEL, PROJ_TN), lambda i, j: (0, col0 + j))],
        out_specs=pl.BlockSpec((None, N_HEADS, tm, LANES), lambda i, j: (j, 0, i, 0)),
        out_shape=jax.ShapeDtypeStruct((n_out, N_HEADS, t, LANES), BF16),
        compiler_params=_cparams("arbitrary", "arbitrary"),
        name="proj_heads",
    )(h, w_in)


def _proj_gates(h, w_in):
    t = h.shape[0]
    tm = _proj_tm(t)
    col0 = (D_SSM + 2 * D_QK + D_ATTN) // PROJ_TN
    return pl.pallas_call(
        _proj_gate_kernel,
        grid=(t // tm, 2 * D_MODEL // PROJ_TN),
        in_specs=[pl.BlockSpec((tm, D_MODEL), lambda i, j: (i, 0)),
                  pl.BlockSpec((D_MODEL, PROJ_TN), lambda i, j: (0, col0 + j))],
        out_specs=pl.BlockSpec((tm, PROJ_TN), lambda i, j: (i, j)),
        out_shape=jax.ShapeDtypeStruct((t, 2 * D_MODEL), BF16),
        compiler_params=_cparams("arbitrary", "arbitrary"),
        name="proj_gates",
    )(h, w_in)


U_TT = 4


def _proj_u_kernel(h_ref, w_ref, z_ref):
    nb = h_ref.shape[0]
    for t in range(U_TT):
        ut = _dot_nt(w_ref[...], h_ref[:, t * D_MODEL:(t + 1) * D_MODEL])
        z_ref[:, t] = ut.reshape(SSM_GROUPS, GROUP_CH, nb).astype(BF16)


def _proj_u(h, w_u_t, n_tok):
    t = h.shape[0]
    n_chunks = t // SSM_CHUNK
    nb = min(MXU_DIM, n_chunks)
    hv = h.reshape(n_chunks, SSM_CHUNK * D_MODEL)
    return pl.pallas_call(
        _proj_u_kernel,
        grid=(n_chunks // nb, SSM_CHUNK // U_TT),
        in_specs=[pl.BlockSpec((nb, U_TT * D_MODEL), lambda i, j: (i, j)),
                  pl.BlockSpec((D_SSM, D_MODEL), lambda i, j: (0, 0))],
        out_specs=pl.BlockSpec((SSM_GROUPS, U_TT, GROUP_CH, nb), lambda i, j: (0, j, 0, i)),
        out_shape=jax.ShapeDtypeStruct((SSM_GROUPS, SSM_CHUNK, GROUP_CH, n_chunks), BF16),
        compiler_params=_cparams("arbitrary", "arbitrary"),
        name="proj_u",
    )(hv, w_u_t)


def _ssm_param_kernel(a_re_ref, a_im_ref, ldt_ref, bt_ref, btsw_ref, cr_ref, ci_ref, d_ref,
                      m_ref, w_ref, v_ref, a_ref):
    tc = SSM_CHUNK
    lane = lax.broadcasted_iota(jnp.int32, (1, STATE_W), 1)
    lo = lane < SSM_STATE
    kts = []
    for d in range(2):
        dt = jnp.exp(ldt_ref[d:d + 1, :])
        lre = a_re_ref[d:d + 1, :] * dt
        lim = a_im_ref[d:d + 1, :] * dt
        kk = lax.broadcasted_iota(jnp.int32, (tc + 1, STATE_W), 0).astype(F32)
        mag = jnp.exp(kk * lre)
        c2 = mag * jnp.cos(kk * lim)
        s2 = mag * jnp.sin(kk * lim)
        xr = c2[1:2] - 1.0
        xi = s2[1:2]
        ar = a_re_ref[d:d + 1, :]
        ai = a_im_ref[d:d + 1, :]
        den = ar * ar + ai * ai
        er = (xr * ar + xi * ai) / den
        ei = (xi * ar - xr * ai) / den
        e2 = jnp.where(lo, -ei, ei)
        bt = bt_ref[d]
        btsw = btsw_ref[d]
        bb = er * bt + e2 * btsw
        bbsw = er * btsw - e2 * bt
        cdr = cr_ref[d]
        cdi = ci_ref[d]
        cneg = jnp.where(lo, cdr, -cdi)
        p2 = jnp.where(lo, -s2, s2)
        q1 = jnp.where(lo, c2, -s2)
        q2 = jnp.where(lo, -s2, -c2)
        gl = []
        for s in range(tc):
            kw = tc - 1 - s if d == 0 else s
            wd = c2[kw:kw + 1] * bb + p2[kw:kw + 1] * bbsw
            wsw = c2[kw:kw + 1] * bbsw - p2[kw:kw + 1] * bb
            w_ref[s, :, (2 * d) * STATE_W:(2 * d + 1) * STATE_W] = wd.astype(w_ref.dtype)
            w_ref[s, :, (2 * d + 1) * STATE_W:(2 * d + 2) * STATE_W] = wsw.astype(w_ref.dtype)
            kv = s + 1 if d == 0 else tc - s
            v_ref[s, :, d * STATE_W:(d + 1) * STATE_W] = (
                cdr * q1[kv:kv + 1] + cdi * q2[kv:kv + 1]).astype(v_ref.dtype)
            kl = tc - 1 - s if d == 0 else s
            gl.append(c2[kl:kl + 1] * bb + p2[kl:kl + 1] * bbsw)
        gcat = jnp.concatenate(gl, axis=0)
        kts.append(_dot_nt(cneg, gcat, precision=lax.Precision.HIGHEST))
        a_ref[d, 0:1, :] = c2[tc:tc + 1]
        a_ref[d, 1:2, :] = p2[tc:tc + 1]
    lane_w = lax.broadcasted_iota(jnp.int32, (GROUP_CH, CHUNK_W), 1)
    row_w = lax.broadcasted_iota(jnp.int32, (GROUP_CH, CHUNK_W), 0)
    dcol = d_ref[...]
    for t in range(tc):
        sh = (CHUNK_W - GROUP_CH * (tc - 1 - t)) % CHUNK_W
        fwd = kts[0] if sh == 0 else pltpu.roll(kts[0], sh, 1)
        fwd = jnp.where(lane_w < GROUP_CH * (t + 1), fwd, 0.0)
        rev = kts[1] if t == 0 else pltpu.roll(kts[1], GROUP_CH * t, 1)
        rev = jnp.where(lane_w >= GROUP_CH * t, rev, 0.0)
        skip = jnp.where(lane_w == GROUP_CH * t + row_w, dcol, 0.0)
        m_ref[t] = (fwd + rev + skip).astype(m_ref.dtype)


def _ssm_params(a_re, a_im, log_dt, b_re, b_im, c_re, c_im, d_skip):
    g = SSM_GROUPS
    dup = lambda a: jnp.concatenate([a, a], axis=-1)
    a_re2 = dup(jnp.transpose(a_re, (1, 0, 2)))
    a_im2 = dup(jnp.transpose(a_im, (1, 0, 2)))
    ldt = jnp.transpose(log_dt, (1, 0))[:, :, None]
    btr = jnp.transpose(b_re, (1, 0, 3, 2))
    bti = jnp.transpose(b_im, (1, 0, 3, 2))
    bt = jnp.concatenate([btr, bti], axis=-1)
    btsw = jnp.concatenate([bti, btr], axis=-1)
    cdr = dup(jnp.transpose(c_re, (1, 0, 2, 3)))
    cdi = dup(jnp.transpose(c_im, (1, 0, 2, 3)))
    dcol = d_skip.reshape(g, GROUP_CH, 1)
    g3 = lambda n: pl.BlockSpec((None, 2, n), lambda i: (i, 0, 0))
    g4 = lambda r, n: pl.BlockSpec((None, 2, r, n), lambda i: (i, 0, 0, 0))
    tc = SSM_CHUNK
    out = lambda n: pl.BlockSpec((None, tc, GROUP_CH, n), lambda i: (i, 0, 0, 0))
    m, w, v, a = pl.pallas_call(
        _ssm_param_kernel,
        grid=(g,),
        in_specs=[g3(STATE_W), g3(STATE_W), g3(1),
                  g4(GROUP_CH, STATE_W), g4(GROUP_CH, STATE_W),
                  g4(GROUP_CH, STATE_W), g4(GROUP_CH, STATE_W),
                  pl.BlockSpec((None, GROUP_CH, 1), lambda i: (i, 0, 0))],
        out_specs=[out(CHUNK_W), out(4 * STATE_W), out(2 * STATE_W),
                   pl.BlockSpec((None, 2, 2, STATE_W), lambda i: (i, 0, 0, 0))],
        out_shape=[jax.ShapeDtypeStruct((g, tc, GROUP_CH, CHUNK_W), BF16),
                   jax.ShapeDtypeStruct((g, tc, GROUP_CH, 4 * STATE_W), BF16),
                   jax.ShapeDtypeStruct((g, tc, GROUP_CH, 2 * STATE_W), BF16),
                   jax.ShapeDtypeStruct((g, 2, 2, STATE_W), F32)],
        compiler_params=_cparams("arbitrary"),
        name="ssm_params",
    )(a_re2, a_im2, ldt, bt, btsw, cdr, cdi, dcol)
    return (m.reshape(g, CHUNK_W, CHUNK_W), w.reshape(g, CHUNK_W, 4 * STATE_W),
            v.reshape(g, CHUNK_W, 2 * STATE_W), a)


ROW_PAD = SUBLANES


def _ssm_kernel(bsz, zl_ref, zc_ref, m_ref, w_ref, v_ref, a_ref, y_ref, sl_ref, sc_ref, h_ref):
    ncl = zl_ref.shape[-1] // bsz
    ncc = zc_ref.shape[-1] // bsz
    pl_l = ncl + ROW_PAD
    pl_c = ncc + ROW_PAD
    zl = zl_ref[...].reshape(CHUNK_W, bsz * ncl)
    zc = zc_ref[...].reshape(CHUNK_W, bsz * ncc)
    w = w_ref[...]
    s_c = _dot_tn(zc, w)
    for b in range(bsz):
        s_l = _dot_tn(zl[:, b * ncl:(b + 1) * ncl], w)
        for k in range(4):
            sl_ref[k, b * pl_l:b * pl_l + ncl, :] = s_l[:, k * STATE_W:(k + 1) * STATE_W]
            sc_ref[k, b * pl_c:b * pl_c + ncc, :] = s_c[b * ncc:(b + 1) * ncc,
                                                        k * STATE_W:(k + 1) * STATE_W]

    a1 = (a_ref[0, 0:1, :], a_ref[1, 0:1, :])
    a2 = (a_ref[0, 1:2, :], a_ref[1, 1:2, :])

    def step(d, h, hsw, s, ssw):
        return a1[d] * h + a2[d] * hsw + s, a1[d] * hsw - a2[d] * h + ssw

    def ctx_body(j, hs):
        hf, hfs, hr, hrs = hs
        rf = pl.ds(j, bsz, stride=pl_c)
        rr = pl.ds(ncc - 1 - j, bsz, stride=pl_c)
        hf, hfs = step(0, hf, hfs, sc_ref[0, rf, :], sc_ref[1, rf, :])
        hr, hrs = step(1, hr, hrs, sc_ref[2, rr, :], sc_ref[3, rr, :])
        return hf, hfs, hr, hrs

    zero = jnp.zeros((bsz, STATE_W), F32)
    hs = lax.fori_loop(0, ncc, ctx_body, (zero, zero, zero, zero))

    def lat_body(j, hs):
        hf, hfs, hr, hrs = hs
        rf = pl.ds(j, bsz, stride=pl_l)
        rr = pl.ds(ncl - 1 - j, bsz, stride=pl_l)
        h_ref[0, rf, :] = hf
        h_ref[1, rr, :] = hr
        hf, hfs = step(0, hf, hfs, sl_ref[0, rf, :], sl_ref[1, rf, :])
        hr, hrs = step(1, hr, hrs, sl_ref[2, rr, :], sl_ref[3, rr, :])
        return hf, hfs, hr, hrs

    lax.fori_loop(0, ncl, lat_body, hs)
    hcat = jnp.concatenate(
        [jnp.concatenate([h_ref[0, b * pl_l:b * pl_l + ncl, :], h_ref[1, b * pl_l:b * pl_l + ncl, :]],
                         axis=1) for b in range(bsz)], axis=0).astype(BF16)
    y = _dot(m_ref[...], zl) + _dot_nt(v_ref[...], hcat)
    y_ref[...] = y.reshape(SSM_CHUNK, GROUP_CH, bsz * ncl).astype(y_ref.dtype)


def _ssm_scan(zl, zc, m, w, v, a, bsz):
    g = zl.shape[0]
    nl = zl.shape[-1]
    nc = zc.shape[-1]
    blk = lambda *s: pl.BlockSpec((None,) + s, lambda i: (i,) + (0,) * len(s))
    return pl.pallas_call(
        functools.partial(_ssm_kernel, bsz),
        grid=(g,),
        in_specs=[blk(SSM_CHUNK, GROUP_CH, nl), blk(SSM_CHUNK, GROUP_CH, nc), blk(CHUNK_W, CHUNK_W),
                  blk(CHUNK_W, 4 * STATE_W), blk(CHUNK_W, 2 * STATE_W), blk(2, 2, STATE_W)],
        out_specs=blk(SSM_CHUNK, GROUP_CH, nl),
        out_shape=jax.ShapeDtypeStruct((g, SSM_CHUNK, GROUP_CH, nl), BF16),
        scratch_shapes=[pltpu.VMEM((4, nl + bsz * ROW_PAD, STATE_W), F32),
                        pltpu.VMEM((4, nc + bsz * ROW_PAD, STATE_W), F32),
                        pltpu.VMEM((2, nl + bsz * ROW_PAD, STATE_W), F32)],
        compiler_params=_cparams("arbitrary"),
        name="ssm_scan",
    )(zl, zc, m, w, v, a)


ATTN_SUB = 64


def _attn_kernel(lam_ref, g_ref, q_ref, kl_ref, vl_ref, kc_ref, vc_ref, o_ref, k_scr, v_scr):
    tq = q_ref.shape[0]
    n_lat = kl_ref.shape[0]
    n_ctx = kc_ref.shape[0]

    @pl.when(pl.program_id(2) == 0)
    def _():
        k_scr[0:n_lat] = kl_ref[...]
        k_scr[n_lat:n_lat + n_ctx] = kc_ref[...]
        v_scr[0:n_lat, 0:V_DIM] = vl_ref[...]
        v_scr[n_lat:n_lat + n_ctx, 0:V_DIM] = vc_ref[...]
        v_scr[:, V_DIM:2 * V_DIM] = jnp.ones((n_lat + n_ctx, V_DIM), BF16)

    lp = lam_ref[...]
    lam = (jnp.exp(jnp.sum(lp[0:1] * lp[1:2], axis=1, keepdims=True))
           - jnp.exp(jnp.sum(lp[2:3] * lp[3:4], axis=1, keepdims=True)) + LAM_INIT)
    gain = g_ref[...] * (1.0 - LAM_INIT)
    lane = lax.broadcasted_iota(jnp.int32, (ATTN_SUB, LANES), 1)
    zero = jnp.zeros((ATTN_SUB, LANES), BF16)
    for c in range(tq // ATTN_SUB):
        q = q_ref[c * ATTN_SUB:(c + 1) * ATTN_SUB, :]
        qq = jnp.concatenate([jnp.where(lane < HEAD_DIM, q, zero),
                              jnp.where(lane >= HEAD_DIM, q, zero)], axis=0)
        s = _dot_nt(qq, k_scr[...])
        mx = jnp.max(s, axis=-1, keepdims=True)
        p = jnp.exp2((s - mx).astype(BF16))
        oe = _dot(p, v_scr[...])
        o2 = oe[:, 0:V_DIM] / oe[:, V_DIM:2 * V_DIM]
        o = o2[0:ATTN_SUB] - lam * o2[ATTN_SUB:2 * ATTN_SUB]
        o = o * lax.rsqrt(jnp.mean(o * o, axis=-1, keepdims=True) + EPS) * gain
        o_ref[c * ATTN_SUB:(c + 1) * ATTN_SUB, :] = o.astype(o_ref.dtype)


def _attention(lam_p, subln_g, qk, v, kvc, bsz, n_lat, n_ctx):
    tq = min(512, n_lat)
    nq = n_lat // tq
    t = bsz * n_lat
    sel = lambda which, rows: pl.BlockSpec((None, None, rows, LANES),
                                           lambda b, h, i: (which, h, b, 0))
    return pl.pallas_call(
        _attn_kernel,
        grid=(bsz, N_HEADS, nq),
        scratch_shapes=[pltpu.VMEM((n_lat + n_ctx, LANES), BF16),
                        pltpu.VMEM((n_lat + n_ctx, 2 * V_DIM), BF16)],
        in_specs=[pl.BlockSpec((4, HEAD_DIM), lambda b, h, i: (0, 0)),
                  pl.BlockSpec((1, V_DIM), lambda b, h, i: (0, 0)),
                  pl.BlockSpec((None, None, tq, LANES), lambda b, h, i: (0, h, b * nq + i, 0)),
                  sel(1, n_lat), sel(0, n_lat), sel(0, n_ctx), sel(1, n_ctx)],
        out_specs=pl.BlockSpec((None, tq, LANES), lambda b, h, i: (h, b * nq + i, 0)),
        out_shape=jax.ShapeDtypeStruct((N_HEADS, t, LANES), BF16),
        compiler_params=_cparams("arbitrary", "arbitrary", "arbitrary"),
        name="diff_attn",
    )(lam_p, subln_g, qk, qk, v, kvc, kvc)


MERGE_TT = 2


def _merge_kernel(y_ref, o_ref, g_ref, x_ref, mod_ref, wg_ref, bg_ref, wps_ref, wpa_ref, wo_ref,
                  out_ref):
    nch = x_ref.shape[0]
    y = jnp.concatenate(
        [y_ref[:, t].reshape(D_SSM, nch).astype(F32).T for t in range(MERGE_TT)], axis=0)
    z = 0.5 * y * (1.0 + jnp.tanh(math.sqrt(2.0 / math.pi) * (y + 0.044715 * (y * y * y))))
    gl = _dot(z.astype(BF16), wg_ref[...]) + bg_ref[...]
    ys = (z * _sigmoid(gl)).astype(BF16)
    oa = jnp.concatenate(
        [jnp.concatenate([o_ref[h, :, t * LANES:(t + 1) * LANES] for h in range(N_HEADS)], axis=1)
         for t in range(MERGE_TT)], axis=0)
    ps = _dot(ys, wps_ref[...])
    pa = _dot(oa, wpa_ref[...])
    gw = 2 * D_MODEL
    merged = jnp.concatenate(
        [g_ref[:, t * gw:t * gw + D_MODEL].astype(F32) * ps[t * nch:(t + 1) * nch]
         + g_ref[:, t * gw + D_MODEL:(t + 1) * gw].astype(F32) * pa[t * nch:(t + 1) * nch]
         for t in range(MERGE_TT)], axis=0)
    out = _dot(merged.astype(BF16), wo_ref[...])
    for t in range(MERGE_TT):
        out_ref[:, t * D_MODEL:(t + 1) * D_MODEL] = (
            x_ref[:, t * D_MODEL:(t + 1) * D_MODEL] + mod_ref[2:3, :] * out[t * nch:(t + 1) * nch])


def _merge(yt, o, gates, x2d, mod, w_glu, b_glu, w_ps, w_pa, w_o, bsz, n_lat):
    t = x2d.shape[0]
    nch = n_lat // SSM_CHUNK
    tt = MERGE_TT
    const = lambda shape: pl.BlockSpec(shape, lambda b, j: (0,) * len(shape),
                                       pipeline_mode=pl.Buffered(1))
    xv = x2d.reshape(bsz * nch, SSM_CHUNK * D_MODEL)
    gv = gates.reshape(bsz * nch, SSM_CHUNK * 2 * D_MODEL)
    ov = o.reshape(N_HEADS, bsz * nch, SSM_CHUNK * LANES)
    out = pl.pallas_call(
        _merge_kernel,
        grid=(bsz, SSM_CHUNK // tt),
        in_specs=[pl.BlockSpec((SSM_GROUPS, tt, GROUP_CH, nch), lambda b, j: (0, j, 0, b)),
                  pl.BlockSpec((N_HEADS, nch, tt * LANES), lambda b, j: (0, b, j)),
                  pl.BlockSpec((nch, tt * 2 * D_MODEL), lambda b, j: (b, j)),
                  pl.BlockSpec((nch, tt * D_MODEL), lambda b, j: (b, j)),
                  pl.BlockSpec((None, 6, D_MODEL), lambda b, j: (b, 0, 0)),
                  const((D_SSM, D_SSM)), const((1, D_SSM)),
                  const((D_SSM, D_MODEL)), const((D_ATTN, D_MODEL)), const((D_MODEL, D_MODEL))],
        out_specs=pl.BlockSpec((nch, tt * D_MODEL), lambda b, j: (b, j)),
        out_shape=jax.ShapeDtypeStruct((bsz * nch, SSM_CHUNK * D_MODEL), F32),
        compiler_params=_cparams("arbitrary", "arbitrary"),
        name="merge",
    )(yt, ov, gv, xv, mod, w_glu, b_glu, w_ps, w_pa, w_o)
    return out.reshape(t, D_MODEL)


def _ffn_kernel(x_ref, mod_ref, g2_ref, gf_ref, wg_ref, wu_ref, wo_ref, out_ref, h_ref, acc_ref):
    j = pl.program_id(1)

    @pl.when(j == 0)
    def _():
        h_ref[...] = _norm_mod(x_ref[...], g2_ref[...], mod_ref[3:4, :], mod_ref[4:5, :]).astype(BF16)
        acc_ref[...] = jnp.zeros_like(acc_ref)

    h = h_ref[...]
    gate = _dot(h, wg_ref[...])
    up = _dot(h, wu_ref[...])
    act = (gate * _sigmoid(gate) * up).astype(BF16)
    acc_ref[...] += _dot(act, wo_ref[...])

    @pl.when(j == pl.num_programs(1) - 1)
    def _():
        xo = x_ref[...] + mod_ref[5:6, :] * acc_ref[...]
        ms = jnp.mean(xo * xo, axis=-1, keepdims=True)
        out_ref[...] = xo * lax.rsqrt(ms + EPS) * gf_ref[...]


def _ffn(x_mid, mod, norm2_g, norm_f_g, w_ffn_in, w_ffn_out, n_lat):
    t = x_mid.shape[0]
    tm = min(512, n_lat)
    tf = 512
    nf = D_FF // tf
    tiles_per_b = n_lat // tm
    row = lambda i, j: (i, 0)
    return pl.pallas_call(
        _ffn_kernel,
        grid=(t // tm, nf),
        in_specs=[pl.BlockSpec((tm, D_MODEL), row),
                  pl.BlockSpec((None, 6, D_MODEL), lambda i, j: (i // tiles_per_b, 0, 0)),
                  pl.BlockSpec((1, D_MODEL), lambda i, j: (0, 0)),
                  pl.BlockSpec((1, D_MODEL), lambda i, j: (0, 0)),
                  pl.BlockSpec((D_MODEL, tf), lambda i, j: (0, j)),
                  pl.BlockSpec((D_MODEL, tf), lambda i, j: (0, j + nf)),
                  pl.BlockSpec((tf, D_MODEL), lambda i, j: (j, 0))],
        out_specs=pl.BlockSpec((tm, D_MODEL), row),
        out_shape=jax.ShapeDtypeStruct((t, D_MODEL), F32),
        scratch_shapes=[pltpu.VMEM((tm, D_MODEL), BF16), pltpu.VMEM((tm, D_MODEL), F32)],
        compiler_params=_cparams("arbitrary", "arbitrary"),
        name="ffn",
    )(x_mid, mod, norm2_g, norm_f_g, w_ffn_in, w_ffn_in, w_ffn_out)


def kernel(x, c, ctx, c_ctx, w_ada, b_ada, norm1_g, w_in, ssm_a_re, ssm_a_im, ssm_log_dt, ssm_b_re, ssm_b_im, ssm_c_re, ssm_c_im, ssm_d, w_glu, b_glu, lambda_q1, lambda_k1, lambda_q2, lambda_k2, subln_g, w_proj_ssm, w_proj_attn, w_out, norm2_g, w_ffn_in, w_ffn_out, norm_f_g):
    bsz, n_lat, d = x.shape
    n_ctx = ctx.shape[1]
    assert d == D_MODEL and w_ada.shape[0] == 1, "single-layer block only"
    assert n_lat % GRID_W == 0 and n_lat % SSM_CHUNK == 0 and n_ctx % SSM_CHUNK == 0
    assert bsz % SUBLANES == 0

    pad = (-(bsz + 1)) % SUBLANES
    cc = jnp.concatenate([c, c_ctx[None, :], jnp.zeros((pad, d), F32)], axis=0)
    mod_all = _adaln(cc, w_ada[0], b_ada[0])
    mod = mod_all[:bsz].reshape(bsz, 6, d)
    mod_c = mod_all[bsz:bsz + 1].reshape(1, 6, d)

    w_in_b = w_in[0].astype(BF16)
    w_u_t = jnp.transpose(w_in[0][:, :D_SSM]).astype(BF16)
    g1 = norm1_g[0].reshape(1, d)
    x2d = x.reshape(bsz * n_lat, d)
    h = _prenorm(x2d, mod, g1, n_lat)
    hc = _prenorm(ctx.reshape(bsz * n_ctx, d), mod_c, g1, n_ctx)

    qk = _proj_rope(h, w_in_b, n_lat)
    v = _proj_heads(h, w_in_b, (D_SSM + 2 * D_QK) // PROJ_TN, 1)
    kvc = _proj_heads(hc, w_in_b, (D_SSM + D_QK) // PROJ_TN, 2)
    gates = _proj_gates(h, w_in_b)
    zl = _proj_u(h, w_u_t, n_lat)
    zc = _proj_u(hc, w_u_t, n_ctx)

    m, w, vv, a = _ssm_params(ssm_a_re[0], ssm_a_im[0], ssm_log_dt[0], ssm_b_re[0], ssm_b_im[0],
                              ssm_c_re[0], ssm_c_im[0], ssm_d[0])
    yt = _ssm_scan(zl, zc, m, w, vv, a, bsz)

    lam_p = jnp.stack([lambda_q1[0], lambda_k1[0], lambda_q2[0], lambda_k2[0]]).astype(F32)
    o = _attention(lam_p, subln_g[0].reshape(1, V_DIM), qk, v, kvc, bsz, n_lat, n_ctx)

    x_mid = _merge(yt, o, gates, x2d, mod, w_glu[0].astype(BF16), b_glu[0].reshape(1, D_SSM),
                   w_proj_ssm[0].astype(BF16), w_proj_attn[0].astype(BF16), w_out[0].astype(BF16),
                   bsz, n_lat)
    out = _ffn(x_mid, mod, norm2_g[0].reshape(1, d), norm_f_g.reshape(1, d),
               w_ffn_in[0].astype(BF16), w_ffn_out[0].astype(BF16), n_lat)
    return out.reshape(bsz, n_lat, d)
```

```python
import functools
import math

import jax
import jax.numpy as jnp
from jax import lax
from jax.experimental import pallas as pl
from jax.experimental.pallas import tpu as pltpu

F32 = jnp.float32
BF16 = jnp.bfloat16

D_MODEL = 2048
GRID_W = 64
D_SSM = 1024
GROUP_CH = 16
SSM_GROUPS = D_SSM // GROUP_CH
SSM_STATE = 64
N_HEADS = 8
HEAD_DIM = 64
V_DIM = 2 * HEAD_DIM
D_QK = N_HEADS * 2 * HEAD_DIM
D_ATTN = N_HEADS * V_DIM
ATTN_SCALE = HEAD_DIM ** -0.5
Q_SCALE = ATTN_SCALE * math.log2(math.e)
ROPE_AXIS_DIM = HEAD_DIM // 2
ROPE_BASE = 10000.0
D_FF = 5632
D_IN = D_SSM + 2 * D_QK + D_ATTN + 2 * D_MODEL
EPS = 1e-6
LAM_INIT = 0.8 - 0.6 * math.exp(-0.3 * 0)

LANES = 128
SUBLANES = 8
MXU_DIM = 256
SSM_CHUNK = 16
CHUNK_W = SSM_CHUNK * GROUP_CH
STATE_W = 2 * SSM_STATE
VMEM_LIMIT = 56 * 1024 * 1024


def _cparams(*sem):
    return pltpu.CompilerParams(dimension_semantics=sem, vmem_limit_bytes=VMEM_LIMIT)


def _dot(a, b):
    return jnp.dot(a, b, preferred_element_type=F32)


def _dot_nt(a, b, precision=None):
    return lax.dot_general(a, b, (((1,), (1,)), ((), ())), preferred_element_type=F32,
                           precision=precision)


def _dot_tn(a, b):
    return lax.dot_general(a, b, (((0,), (0,)), ((), ())), preferred_element_type=F32)


def _sigmoid(x):
    return 1.0 / (1.0 + jnp.exp(-x))


def _adaln_kernel(c_ref, w_ref, b_ref, o_ref):
    c = c_ref[...]
    s = (c * _sigmoid(c)).astype(BF16)
    o_ref[...] = _dot(s, w_ref[...].astype(BF16)) + b_ref[...]


def _adaln(cc, w_ada, b_ada):
    rows, d = cc.shape
    n = w_ada.shape[1]
    tn = 1024
    return pl.pallas_call(
        _adaln_kernel,
        grid=(n // tn,),
        in_specs=[pl.BlockSpec((rows, d), lambda j: (0, 0)),
                  pl.BlockSpec((d, tn), lambda j: (0, j)),
                  pl.BlockSpec((1, tn), lambda j: (0, j))],
        out_specs=pl.BlockSpec((rows, tn), lambda j: (0, j)),
        out_shape=jax.ShapeDtypeStruct((rows, n), F32),
        compiler_params=_cparams("arbitrary"),
        name="adaln",
    )(cc, w_ada, b_ada.reshape(1, n))


def _norm_mod(x, g, shift, scale):
    ms = jnp.mean(x * x, axis=-1, keepdims=True)
    y = x * lax.rsqrt(ms + EPS) * g
    return y * (1.0 + scale) + shift


def _prenorm_kernel(x_ref, mod_ref, g_ref, h_ref):
    h_ref[...] = _norm_mod(x_ref[...], g_ref[...], mod_ref[0:1, :], mod_ref[1:2, :]).astype(BF16)


def _prenorm(x2d, mod, norm_g, n_tok):
    t = x2d.shape[0]
    tm = min(512, n_tok)
    tiles_per_b = n_tok // tm
    per_batch = mod.shape[0] > 1
    return pl.pallas_call(
        _prenorm_kernel,
        grid=(t // tm,),
        in_specs=[pl.BlockSpec((tm, D_MODEL), lambda i: (i, 0)),
                  pl.BlockSpec((None, 6, D_MODEL),
                               (lambda i: (i // tiles_per_b, 0, 0)) if per_batch else (lambda i: (0, 0, 0))),
                  pl.BlockSpec((1, D_MODEL), lambda i: (0, 0))],
        out_specs=pl.BlockSpec((tm, D_MODEL), lambda i: (i, 0)),
        out_shape=jax.ShapeDtypeStruct((t, D_MODEL), BF16),
        compiler_params=_cparams("arbitrary"),
        name="prenorm",
    )(x2d, mod, norm_g)


PROJ_TN = 1024


def _rope_tables(n_lat):
    pos = jnp.arange(n_lat, dtype=jnp.int32)
    row = (pos // GRID_W).astype(F32)
    col = (pos % GRID_W).astype(F32)
    inv = 1.0 / (ROPE_BASE ** (jnp.arange(0, ROPE_AXIS_DIM, 2, dtype=F32) / ROPE_AXIS_DIM))
    lane = jnp.arange(LANES)
    d = lane % HEAD_DIM
    freq = inv[d % (ROPE_AXIS_DIM // 2)]
    ang = jnp.where(d < ROPE_AXIS_DIM, row[:, None], col[:, None]) * freq[None, :]
    first = (lane % ROPE_AXIS_DIM) < (ROPE_AXIS_DIM // 2)
    cos, sin = jnp.cos(ang), jnp.sin(ang)
    return cos, jnp.where(first, -sin, 0.0), jnp.where(first, 0.0, sin)


def _proj_rope_kernel(h_ref, w_ref, cos_ref, sa_ref, sb_ref, o_ref):
    scale = jnp.where(pl.program_id(1) == 0, Q_SCALE, 1.0).astype(F32)
    cos = cos_ref[...] * scale
    sa = sa_ref[...] * scale
    sb = sb_ref[...] * scale
    for c in range(PROJ_TN // MXU_DIM):
        acc = _dot(h_ref[...], w_ref[:, c * MXU_DIM:(c + 1) * MXU_DIM])
        for hh in range(MXU_DIM // LANES):
            a = acc[:, hh * LANES:(hh + 1) * LANES]
            r = a * cos + pltpu.roll(a, LANES - 16, 1) * sa + pltpu.roll(a, 16, 1) * sb
            o_ref[c * (MXU_DIM // LANES) + hh] = r.astype(BF16)


def _proj_heads_kernel(h_ref, w_ref, o_ref):
    for c in range(PROJ_TN // MXU_DIM):
        acc = _dot(h_ref[...], w_ref[:, c * MXU_DIM:(c + 1) * MXU_DIM])
        for hh in range(MXU_DIM // LANES):
            o_ref[c * (MXU_DIM // LANES) + hh] = acc[:, hh * LANES:(hh + 1) * LANES].astype(BF16)


def _proj_gate_kernel(h_ref, w_ref, o_ref):
    for c in range(PROJ_TN // MXU_DIM):
        acc = _dot(h_ref[...], w_ref[:, c * MXU_DIM:(c + 1) * MXU_DIM])
        o_ref[:, c * MXU_DIM:(c + 1) * MXU_DIM] = _sigmoid(acc).astype(BF16)


def _proj_tm(t):
    return min(1024, t)


def _proj_rope(h, w_in, n_lat):
    t = h.shape[0]
    tm = min(_proj_tm(t), n_lat)
    tiles_per_b = n_lat // tm
    tab = pl.BlockSpec((tm, LANES), lambda i, j: (i % tiles_per_b, 0))
    col0 = D_SSM // PROJ_TN
    return pl.pallas_call(
        _proj_rope_kernel,
        grid=(t // tm, 2),
        in_specs=[pl.BlockSpec((tm, D_MODEL), lambda i, j: (i, 0)),
                  pl.BlockSpec((D_MODEL, PROJ_TN), lambda i, j: (0, col0 + j)),
                  tab, tab, tab],
        out_specs=pl.BlockSpec((None, N_HEADS, tm, LANES), lambda i, j: (j, 0, i, 0)),
        out_shape=jax.ShapeDtypeStruct((2, N_HEADS, t, LANES), BF16),
        compiler_params=_cparams("arbitrary", "arbitrary"),
        name="proj_qk",
    )(h, w_in, *_rope_tables(n_lat))


def _proj_heads(h, w_in, col0, n_out):
    t = h.shape[0]
    tm = _proj_tm(t)
    return pl.pallas_call(
        _proj_heads_kernel,
        grid=(t // tm, n_out),
        in_specs=[pl.BlockSpec((tm, D_MODEL), lambda i, j: (i, 0)),
                  pl.BlockSpec((D_MODEL, PROJ_TN), lambda i, j: (0, col0 + j))],
        out_specs=pl.BlockSpec((None, N_HEADS, tm, LANES), lambda i, j: (j, 0, i, 0)),
        out_shape=jax.ShapeDtypeStruct((n_out, N_HEADS, t, LANES), BF16),
        compiler_params=_cparams("arbitrary", "arbitrary"),
        name="proj_heads",
    )(h, w_in)


def _proj_gates(h, w_in):
    t = h.shape[0]
    tm = _proj_tm(t)
    col0 = (D_SSM + 2 * D_QK + D_ATTN) // PROJ_TN
    return pl.pallas_call(
        _proj_gate_kernel,
        grid=(t // tm, 2 * D_MODEL // PROJ_TN),
        in_specs=[pl.BlockSpec((tm, D_MODEL), lambda i, j: (i, 0)),
                  pl.BlockSpec((D_MODEL, PROJ_TN), lambda i, j: (0, col0 + j))],
        out_specs=pl.BlockSpec((tm, PROJ_TN), lambda i, j: (i, j)),
        out_shape=jax.ShapeDtypeStruct((t, 2 * D_MODEL), BF16),
        compiler_params=_cparams("arbitrary", "arbitrary"),
        name="proj_gates",
    )(h, w_in)


U_ROWS = 512
GROUPS_PER_TILE = LANES // GROUP_CH


def _proj_u_kernel(h_ref, w_ref, z_ref, u_ref):
    n_tok = h_ref.shape[0]
    nch = n_tok // SSM_CHUNK
    for r in range(n_tok // U_ROWS):
        rows = slice(r * U_ROWS, (r + 1) * U_ROWS)
        u = _dot(h_ref[rows, :], w_ref[...])
        for lt in range(D_SSM // LANES):
            u_ref[lt, rows, :] = u[:, lt * LANES:(lt + 1) * LANES]
    for t in range(SSM_CHUNK):
        for lt in range(D_SSM // LANES):
            comb = u_ref[lt, pl.ds(t, nch, stride=SSM_CHUNK), :]
            z_ref[lt * GROUPS_PER_TILE:(lt + 1) * GROUPS_PER_TILE, t] = (
                comb.T.reshape(GROUPS_PER_TILE, GROUP_CH, nch).astype(BF16))


def _proj_u(h, w_u):
    t = h.shape[0]
    tm = min(SSM_CHUNK * LANES, t)
    nch = tm // SSM_CHUNK
    return pl.pallas_call(
        _proj_u_kernel,
        grid=(t // tm,),
        in_specs=[pl.BlockSpec((tm, D_MODEL), lambda i: (i, 0)),
                  pl.BlockSpec((D_MODEL, D_SSM), lambda i: (0, 0), pipeline_mode=pl.Buffered(1))],
        out_specs=pl.BlockSpec((SSM_GROUPS, SSM_CHUNK, GROUP_CH, nch), lambda i: (0, 0, 0, i)),
        out_shape=jax.ShapeDtypeStruct((SSM_GROUPS, SSM_CHUNK, GROUP_CH, t // SSM_CHUNK), BF16),
        scratch_shapes=[pltpu.VMEM((D_SSM // LANES, tm, LANES), F32)],
        compiler_params=_cparams("arbitrary"),
        name="proj_u",
    )(h, w_u)


def _ssm_param_kernel(a_re_ref, a_im_ref, ldt_ref, bt_ref, btsw_ref, cr_ref, ci_ref, d_ref,
                      m_ref, w_ref, v_ref, a_ref):
    tc = SSM_CHUNK
    lane = lax.broadcasted_iota(jnp.int32, (1, STATE_W), 1)
    lo = lane < SSM_STATE
    kts = []
    for d in range(2):
        dt = jnp.exp(ldt_ref[d:d + 1, :])
        lre = a_re_ref[d:d + 1, :] * dt
        lim = a_im_ref[d:d + 1, :] * dt
        kk = lax.broadcasted_iota(jnp.int32, (tc + 1, STATE_W), 0).astype(F32)
        mag = jnp.exp(kk * lre)
        c2 = mag * jnp.cos(kk * lim)
        s2 = mag * jnp.sin(kk * lim)
        xr = c2[1:2] - 1.0
        xi = s2[1:2]
        ar = a_re_ref[d:d + 1, :]
        ai = a_im_ref[d:d + 1, :]
        den = ar * ar + ai * ai
        er = (xr * ar + xi * ai) / den
        ei = (xi * ar - xr * ai) / den
        e2 = jnp.where(lo, -ei, ei)
        bt = bt_ref[d]
        btsw = btsw_ref[d]
        bb = er * bt + e2 * btsw
        bbsw = er * btsw - e2 * bt
        cdr = cr_ref[d]
        cdi = ci_ref[d]
        cneg = jnp.where(lo, cdr, -cdi)
        p2 = jnp.where(lo, -s2, s2)
        q1 = jnp.where(lo, c2, -s2)
        q2 = jnp.where(lo, -s2, -c2)
        gl = []
        for s in range(tc):
            kw = tc - 1 - s if d == 0 else s
            wd = c2[kw:kw + 1] * bb + p2[kw:kw + 1] * bbsw
            wsw = c2[kw:kw + 1] * bbsw - p2[kw:kw + 1] * bb
            w_ref[s, :, (2 * d) * STATE_W:(2 * d + 1) * STATE_W] = wd.astype(w_ref.dtype)
            w_ref[s, :, (2 * d + 1) * STATE_W:(2 * d + 2) * STATE_W] = wsw.astype(w_ref.dtype)
            kv = s + 1 if d == 0 else tc - s
            v_ref[s, :, d * STATE_W:(d + 1) * STATE_W] = (
                cdr * q1[kv:kv + 1] + cdi * q2[kv:kv + 1]).astype(v_ref.dtype)
            kl = tc - 1 - s if d == 0 else s
            gl.append(c2[kl:kl + 1] * bb + p2[kl:kl + 1] * bbsw)
        gcat = jnp.concatenate(gl, axis=0)
        kts.append(_dot_nt(cneg, gcat, precision=lax.Precision.HIGHEST))
        a_ref[d, 0:1, :] = c2[tc:tc + 1]
        a_ref[d, 1:2, :] = p2[tc:tc + 1]
    lane_w = lax.broadcasted_iota(jnp.int32, (GROUP_CH, CHUNK_W), 1)
    row_w = lax.broadcasted_iota(jnp.int32, (GROUP_CH, CHUNK_W), 0)
    dcol = d_ref[...]
    for t in range(tc):
        sh = (CHUNK_W - GROUP_CH * (tc - 1 - t)) % CHUNK_W
        fwd = kts[0] if sh == 0 else pltpu.roll(kts[0], sh, 1)
        fwd = jnp.where(lane_w < GROUP_CH * (t + 1), fwd, 0.0)
        rev = kts[1] if t == 0 else pltpu.roll(kts[1], GROUP_CH * t, 1)
        rev = jnp.where(lane_w >= GROUP_CH * t, rev, 0.0)
        skip = jnp.where(lane_w == GROUP_CH * t + row_w, dcol, 0.0)
        m_ref[t] = (fwd + rev + skip).astype(m_ref.dtype)


def _ssm_params(a_re, a_im, log_dt, b_re, b_im, c_re, c_im, d_skip):
    g = SSM_GROUPS
    dup = lambda a: jnp.concatenate([a, a], axis=-1)
    a_re2 = dup(jnp.transpose(a_re, (1, 0, 2)))
    a_im2 = dup(jnp.transpose(a_im, (1, 0, 2)))
    ldt = jnp.transpose(log_dt, (1, 0))[:, :, None]
    btr = jnp.transpose(b_re, (1, 0, 3, 2))
    bti = jnp.transpose(b_im, (1, 0, 3, 2))
    bt = jnp.concatenate([btr, bti], axis=-1)
    btsw = jnp.concatenate([bti, btr], axis=-1)
    cdr = dup(jnp.transpose(c_re, (1, 0, 2, 3)))
    cdi = dup(jnp.transpose(c_im, (1, 0, 2, 3)))
    dcol = d_skip.reshape(g, GROUP_CH, 1)
    g3 = lambda n: pl.BlockSpec((None, 2, n), lambda i: (i, 0, 0))
    g4 = lambda r, n: pl.BlockSpec((None, 2, r, n), lambda i: (i, 0, 0, 0))
    tc = SSM_CHUNK
    out = lambda n: pl.BlockSpec((None, tc, GROUP_CH, n), lambda i: (i, 0, 0, 0))
    m, w, v, a = pl.pallas_call(
        _ssm_param_kernel,
        grid=(g,),
        in_specs=[g3(STATE_W), g3(STATE_W), g3(1),
                  g4(GROUP_CH, STATE_W), g4(GROUP_CH, STATE_W),
                  g4(GROUP_CH, STATE_W), g4(GROUP_CH, STATE_W),
                  pl.BlockSpec((None, GROUP_CH, 1), lambda i: (i, 0, 0))],
        out_specs=[out(CHUNK_W), out(4 * STATE_W), out(2 * STATE_W),
                   pl.BlockSpec((None, 2, 2, STATE_W), lambda i: (i, 0, 0, 0))],
        out_shape=[jax.ShapeDtypeStruct((g, tc, GROUP_CH, CHUNK_W), BF16),
                   jax.ShapeDtypeStruct((g, tc, GROUP_CH, 4 * STATE_W), BF16),
                   jax.ShapeDtypeStruct((g, tc, GROUP_CH, 2 * STATE_W), BF16),
                   jax.ShapeDtypeStruct((g, 2, 2, STATE_W), F32)],
        compiler_params=_cparams("arbitrary"),
        name="ssm_params",
    )(a_re2, a_im2, ldt, bt, btsw, cdr, cdi, dcol)
    return (m.reshape(g, CHUNK_W, CHUNK_W), w.reshape(g, CHUNK_W, 4 * STATE_W),
            v.reshape(g, CHUNK_W, 2 * STATE_W), a)


ROW_PAD = SUBLANES


def _ssm_kernel(bsz, zl_ref, zc_ref, m_ref, w_ref, v_ref, a_ref, y_ref, sl_ref, sc_ref, h_ref):
    ncl = zl_ref.shape[-1] // bsz
    ncc = zc_ref.shape[-1] // bsz
    pl_l = ncl + ROW_PAD
    pl_c = ncc + ROW_PAD
    zl = zl_ref[...].reshape(CHUNK_W, bsz * ncl)
    zc = zc_ref[...].reshape(CHUNK_W, bsz * ncc)
    w = w_ref[...]
    s_c = _dot_tn(zc, w)
    for b in range(bsz):
        s_l = _dot_tn(zl[:, b * ncl:(b + 1) * ncl], w)
        for k in range(4):
            sl_ref[k, b * pl_l:b * pl_l + ncl, :] = s_l[:, k * STATE_W:(k + 1) * STATE_W]
            sc_ref[k, b * pl_c:b * pl_c + ncc, :] = s_c[b * ncc:(b + 1) * ncc,
                                                        k * STATE_W:(k + 1) * STATE_W]

    a1 = (a_ref[0, 0:1, :], a_ref[1, 0:1, :])
    a2 = (a_ref[0, 1:2, :], a_ref[1, 1:2, :])

    def step(d, h, hsw, s, ssw):
        return a1[d] * h + a2[d] * hsw + s, a1[d] * hsw - a2[d] * h + ssw

    def ctx_body(j, hs):
        hf, hfs, hr, hrs = hs
        rf = pl.ds(j, bsz, stride=pl_c)
        rr = pl.ds(ncc - 1 - j, bsz, stride=pl_c)
        hf, hfs = step(0, hf, hfs, sc_ref[0, rf, :], sc_ref[1, rf, :])
        hr, hrs = step(1, hr, hrs, sc_ref[2, rr, :], sc_ref[3, rr, :])
        return hf, hfs, hr, hrs

    zero = jnp.zeros((bsz, STATE_W), F32)
    hs = lax.fori_loop(0, ncc, ctx_body, (zero, zero, zero, zero))

    def lat_body(j, hs):
        hf, hfs, hr, hrs = hs
        rf = pl.ds(j, bsz, stride=pl_l)
        rr = pl.ds(ncl - 1 - j, bsz, stride=pl_l)
        h_ref[0, rf, :] = hf
        h_ref[1, rr, :] = hr
        hf, hfs = step(0, hf, hfs, sl_ref[0, rf, :], sl_ref[1, rf, :])
        hr, hrs = step(1, hr, hrs, sl_ref[2, rr, :], sl_ref[3, rr, :])
        return hf, hfs, hr, hrs

    lax.fori_loop(0, ncl, lat_body, hs)
    hcat = jnp.concatenate(
        [jnp.concatenate([h_ref[0, b * pl_l:b * pl_l + ncl, :], h_ref[1, b * pl_l:b * pl_l + ncl, :]],
                         axis=1) for b in range(bsz)], axis=0).astype(BF16)
    y = _dot(m_ref[...], zl) + _dot_nt(v_ref[...], hcat)
    y_ref[...] = y.reshape(SSM_CHUNK, GROUP_CH, bsz * ncl).astype(y_ref.dtype)


def _ssm_scan(zl, zc, m, w, v, a, bsz):
    g = zl.shape[0]
    nl = zl.shape[-1]
    nc = zc.shape[-1]
    blk = lambda *s: pl.BlockSpec((None,) + s, lambda i: (i,) + (0,) * len(s))
    return pl.pallas_call(
        functools.partial(_ssm_kernel, bsz),
        grid=(g,),
        in_specs=[blk(SSM_CHUNK, GROUP_CH, nl), blk(SSM_CHUNK, GROUP_CH, nc), blk(CHUNK_W, CHUNK_W),
                  blk(CHUNK_W, 4 * STATE_W), blk(CHUNK_W, 2 * STATE_W), blk(2, 2, STATE_W)],
        out_specs=blk(SSM_CHUNK, GROUP_CH, nl),
        out_shape=jax.ShapeDtypeStruct((g, SSM_CHUNK, GROUP_CH, nl), BF16),
        scratch_shapes=[pltpu.VMEM((4, nl + bsz * ROW_PAD, STATE_W), F32),
                        pltpu.VMEM((4, nc + bsz * ROW_PAD, STATE_W), F32),
                        pltpu.VMEM((2, nl + bsz * ROW_PAD, STATE_W), F32)],
        compiler_params=_cparams("arbitrary"),
        name="ssm_scan",
    )(zl, zc, m, w, v, a)


ATTN_SUB = 64


def _attn_kernel(lam_ref, g_ref, q_ref, kl_ref, vl_ref, kc_ref, vc_ref, o_ref, k_scr, v_scr):
    tq = q_ref.shape[0]
    n_lat = kl_ref.shape[0]
    n_ctx = kc_ref.shape[0]

    @pl.when(pl.program_id(2) == 0)
    def _():
        k_scr[0:n_lat] = kl_ref[...]
        k_scr[n_lat:n_lat + n_ctx] = kc_ref[...]
        v_scr[0:n_lat, 0:V_DIM] = vl_ref[...]
        v_scr[n_lat:n_lat + n_ctx, 0:V_DIM] = vc_ref[...]
        v_scr[:, V_DIM:2 * V_DIM] = jnp.ones((n_lat + n_ctx, V_DIM), BF16)

    lp = lam_ref[...]
    lam = (jnp.exp(jnp.sum(lp[0:1] * lp[1:2], axis=1, keepdims=True))
           - jnp.exp(jnp.sum(lp[2:3] * lp[3:4], axis=1, keepdims=True)) + LAM_INIT)
    gain = g_ref[...] * (1.0 - LAM_INIT)
    lane = lax.broadcasted_iota(jnp.int32, (ATTN_SUB, LANES), 1)
    zero = jnp.zeros((ATTN_SUB, LANES), BF16)
    for c in range(tq // ATTN_SUB):
        q = q_ref[c * ATTN_SUB:(c + 1) * ATTN_SUB, :]
        qq = jnp.concatenate([jnp.where(lane < HEAD_DIM, q, zero),
                              jnp.where(lane >= HEAD_DIM, q, zero)], axis=0)
        s = _dot_nt(qq, k_scr[...])
        mx = jnp.max(s, axis=-1, keepdims=True)
        p = jnp.exp2((s - mx).astype(BF16))
        oe = _dot(p, v_scr[...])
        o2 = oe[:, 0:V_DIM] / oe[:, V_DIM:2 * V_DIM]
        o = o2[0:ATTN_SUB] - lam * o2[ATTN_SUB:2 * ATTN_SUB]
        o = o * lax.rsqrt(jnp.mean(o * o, axis=-1, keepdims=True) + EPS) * gain
        o_ref[c * ATTN_SUB:(c + 1) * ATTN_SUB, :] = o.astype(o_ref.dtype)


def _attention(lam_p, subln_g, qk, v, kvc, bsz, n_lat, n_ctx):
    tq = min(512, n_lat)
    nq = n_lat // tq
    t = bsz * n_lat
    sel = lambda which, rows: pl.BlockSpec((None, None, rows, LANES),
                                           lambda b, h, i: (which, h, b, 0))
    return pl.pallas_call(
        _attn_kernel,
        grid=(bsz, N_HEADS, nq),
        scratch_shapes=[pltpu.VMEM((n_lat + n_ctx, LANES), BF16),
                        pltpu.VMEM((n_lat + n_ctx, 2 * V_DIM), BF16)],
        in_specs=[pl.BlockSpec((4, HEAD_DIM), lambda b, h, i: (0, 0)),
                  pl.BlockSpec((1, V_DIM), lambda b, h, i: (0, 0)),
                  pl.BlockSpec((None, None, tq, LANES), lambda b, h, i: (0, h, b * nq + i, 0)),
                  sel(1, n_lat), sel(0, n_lat), sel(0, n_ctx), sel(1, n_ctx)],
        out_specs=pl.BlockSpec((None, tq, LANES), lambda b, h, i: (h, b * nq + i, 0)),
        out_shape=jax.ShapeDtypeStruct((N_HEADS, t, LANES), BF16),
        compiler_params=_cparams("arbitrary", "arbitrary", "arbitrary"),
        name="diff_attn",
    )(lam_p, subln_g, qk, qk, v, kvc, kvc)


def _unchunk_kernel(y_ref, o_ref, nat_ref):
    nch = y_ref.shape[-1]
    for t in range(SSM_CHUNK):
        yt = y_ref[:, t].reshape(D_SSM, nch).astype(F32).T
        for lt in range(D_SSM // LANES):
            nat_ref[lt, pl.ds(t, nch, stride=SSM_CHUNK), :] = yt[:, lt * LANES:(lt + 1) * LANES]
    for lt in range(D_SSM // LANES):
        o_ref[:, lt * LANES:(lt + 1) * LANES] = nat_ref[lt].astype(o_ref.dtype)


def _unchunk(yt, bsz, n_lat):
    nch = n_lat // SSM_CHUNK
    return pl.pallas_call(
        _unchunk_kernel,
        grid=(bsz,),
        in_specs=[pl.BlockSpec((SSM_GROUPS, SSM_CHUNK, GROUP_CH, nch), lambda b: (0, 0, 0, b))],
        out_specs=pl.BlockSpec((n_lat, D_SSM), lambda b: (b, 0)),
        out_shape=jax.ShapeDtypeStruct((bsz * n_lat, D_SSM), BF16),
        scratch_shapes=[pltpu.VMEM((D_SSM // LANES, n_lat, LANES), F32)],
        compiler_params=_cparams("arbitrary"),
        name="unchunk",
    )(yt)


MERGE_SUB = 256


def _merge_kernel(y_ref, o_ref, g_ref, x_ref, mod_ref, wg_ref, bg_ref, wps_ref, wpa_ref, wo_ref,
                  out_ref):
    for r in range(x_ref.shape[0] // MERGE_SUB):
        rows = slice(r * MERGE_SUB, (r + 1) * MERGE_SUB)
        y = y_ref[rows, :].astype(F32)
        z = 0.5 * y * (1.0 + jnp.tanh(math.sqrt(2.0 / math.pi) * (y + 0.044715 * (y * y * y))))
        gl = _dot(z.astype(BF16), wg_ref[...]) + bg_ref[...]
        ys = (z * _sigmoid(gl)).astype(BF16)
        oa = jnp.concatenate([o_ref[h, rows, :] for h in range(N_HEADS)], axis=1)
        merged = (g_ref[rows, 0:D_MODEL].astype(F32) * _dot(ys, wps_ref[...])
                  + g_ref[rows, D_MODEL:2 * D_MODEL].astype(F32) * _dot(oa, wpa_ref[...]))
        out = _dot(merged.astype(BF16), wo_ref[...])
        out_ref[rows, :] = x_ref[rows, :] + mod_ref[2:3, :] * out


def _merge(y, o, gates, x2d, mod, w_glu, b_glu, w_ps, w_pa, w_o, n_lat):
    t = x2d.shape[0]
    tm = min(512, n_lat)
    tiles_per_b = n_lat // tm
    row = lambda i: (i, 0)
    const = lambda shape: pl.BlockSpec(shape, lambda i: (0,) * len(shape),
                                       pipeline_mode=pl.Buffered(1))
    return pl.pallas_call(
        _merge_kernel,
        grid=(t // tm,),
        in_specs=[pl.BlockSpec((tm, D_SSM), row),
                  pl.BlockSpec((N_HEADS, tm, LANES), lambda i: (0, i, 0)),
                  pl.BlockSpec((tm, 2 * D_MODEL), row),
                  pl.BlockSpec((tm, D_MODEL), row),
                  pl.BlockSpec((None, 6, D_MODEL), lambda i: (i // tiles_per_b, 0, 0)),
                  const((D_SSM, D_SSM)), const((1, D_SSM)),
                  const((D_SSM, D_MODEL)), const((D_ATTN, D_MODEL)), const((D_MODEL, D_MODEL))],
        out_specs=pl.BlockSpec((tm, D_MODEL), row),
        out_shape=jax.ShapeDtypeStruct((t, D_MODEL), F32),
        compiler_params=_cparams("arbitrary"),
        name="merge",
    )(y, o, gates, x2d, mod, w_glu, b_glu, w_ps, w_pa, w_o)


FFN_SUB = 256


def _ffn_kernel(x_ref, mod_ref, g2_ref, gf_ref, wg_ref, wu_ref, wo_ref, out_ref, h_ref):
    j = pl.program_id(1)
    last = pl.num_programs(1) - 1

    def chains(first, final):
        for r in range(x_ref.shape[0] // FFN_SUB):
            rows = slice(r * FFN_SUB, (r + 1) * FFN_SUB)
            if first:
                h_ref[rows, :] = _norm_mod(x_ref[rows, :], g2_ref[...], mod_ref[3:4, :],
                                           mod_ref[4:5, :]).astype(BF16)
            h = h_ref[rows, :]
            gate = _dot(h, wg_ref[...])
            up = _dot(h, wu_ref[...])
            act = (gate * _sigmoid(gate) * up).astype(BF16)
            part = _dot(act, wo_ref[...])
            if first:
                out_ref[rows, :] = part
            elif not final:
                out_ref[rows, :] += part
            else:
                xo = x_ref[rows, :] + mod_ref[5:6, :] * (out_ref[rows, :] + part)
                ms = jnp.mean(xo * xo, axis=-1, keepdims=True)
                out_ref[rows, :] = xo * lax.rsqrt(ms + EPS) * gf_ref[...]

    pl.when(j == 0)(lambda: chains(True, False))
    pl.when((j > 0) & (j < last))(lambda: chains(False, False))
    pl.when(j == last)(lambda: chains(False, True))


def _ffn(x_mid, mod, norm2_g, norm_f_g, w_ffn_in, w_ffn_out, n_lat):
    t = x_mid.shape[0]
    tm = min(512, n_lat)
    tf = 512
    nf = D_FF // tf
    assert nf >= 2, "the kernel needs distinct first and last hidden-dimension steps"
    tiles_per_b = n_lat // tm
    row = lambda i, j: (i, 0)
    return pl.pallas_call(
        _ffn_kernel,
        grid=(t // tm, nf),
        in_specs=[pl.BlockSpec((tm, D_MODEL), row),
                  pl.BlockSpec((None, 6, D_MODEL), lambda i, j: (i // tiles_per_b, 0, 0)),
                  pl.BlockSpec((1, D_MODEL), lambda i, j: (0, 0)),
                  pl.BlockSpec((1, D_MODEL), lambda i, j: (0, 0)),
                  pl.BlockSpec((D_MODEL, tf), lambda i, j: (0, j)),
                  pl.BlockSpec((D_MODEL, tf), lambda i, j: (0, j + nf)),
                  pl.BlockSpec((tf, D_MODEL), lambda i, j: (j, 0))],
        out_specs=pl.BlockSpec((tm, D_MODEL), row),
        out_shape=jax.ShapeDtypeStruct((t, D_MODEL), F32),
        scratch_shapes=[pltpu.VMEM((tm, D_MODEL), BF16)],
        compiler_params=_cparams("arbitrary", "arbitrary"),
        name="ffn",
    )(x_mid, mod, norm2_g, norm_f_g, w_ffn_in, w_ffn_in, w_ffn_out)


def kernel(x, c, ctx, c_ctx, w_ada, b_ada, norm1_g, w_in, ssm_a_re, ssm_a_im, ssm_log_dt, ssm_b_re, ssm_b_im, ssm_c_re, ssm_c_im, ssm_d, w_glu, b_glu, lambda_q1, lambda_k1, lambda_q2, lambda_k2, subln_g, w_proj_ssm, w_proj_attn, w_out, norm2_g, w_ffn_in, w_ffn_out, norm_f_g):
    bsz, n_lat, d = x.shape
    n_ctx = ctx.shape[1]
    assert d == D_MODEL and w_ada.shape[0] == 1, "single-layer block only"
    assert n_lat % GRID_W == 0 and n_lat % SSM_CHUNK == 0 and n_ctx % SSM_CHUNK == 0
    assert bsz % SUBLANES == 0

    pad = (-(bsz + 1)) % SUBLANES
    cc = jnp.concatenate([c, c_ctx[None, :], jnp.zeros((pad, d), F32)], axis=0)
    mod_all = _adaln(cc, w_ada[0], b_ada[0])
    mod = mod_all[:bsz].reshape(bsz, 6, d)
    mod_c = mod_all[bsz:bsz + 1].reshape(1, 6, d)

    w_in_b = w_in[0].astype(BF16)
    g1 = norm1_g[0].reshape(1, d)
    x2d = x.reshape(bsz * n_lat, d)
    h = _prenorm(x2d, mod, g1, n_lat)
    hc = _prenorm(ctx.reshape(bsz * n_ctx, d), mod_c, g1, n_ctx)

    qk = _proj_rope(h, w_in_b, n_lat)
    v = _proj_heads(h, w_in_b, (D_SSM + 2 * D_QK) // PROJ_TN, 1)
    kvc = _proj_heads(hc, w_in_b, (D_SSM + D_QK) // PROJ_TN, 2)
    gates = _proj_gates(h, w_in_b)
    zl = _proj_u(h, w_in_b)
    zc = _proj_u(hc, w_in_b)

    m, w, vv, a = _ssm_params(ssm_a_re[0], ssm_a_im[0], ssm_log_dt[0], ssm_b_re[0], ssm_b_im[0],
                              ssm_c_re[0], ssm_c_im[0], ssm_d[0])
    yt = _ssm_scan(zl, zc, m, w, vv, a, bsz)

    lam_p = jnp.stack([lambda_q1[0], lambda_k1[0], lambda_q2[0], lambda_k2[0]]).astype(F32)
    o = _attention(lam_p, subln_g[0].reshape(1, V_DIM), qk, v, kvc, bsz, n_lat, n_ctx)

    y = _unchunk(yt, bsz, n_lat)
    x_mid = _merge(y, o, gates, x2d, mod, w_glu[0].astype(BF16), b_glu[0].reshape(1, D_SSM),
                   w_proj_ssm[0].astype(BF16), w_proj_attn[0].astype(BF16), w_out[0].astype(BF16),
                   n_lat)
    out = _ffn(x_mid, mod, norm2_g[0].reshape(1, d), norm_f_g.reshape(1, d),
               w_ffn_in[0].astype(BF16), w_ffn_out[0].astype(BF16), n_lat)
    return out.reshape(bsz, n_lat, d)
```

```python
import functools
import math

import jax
import jax.numpy as jnp
from jax import lax
from jax.experimental import pallas as pl
from jax.experimental.pallas import tpu as pltpu

F32 = jnp.float32
BF16 = jnp.bfloat16

D_MODEL = 2048
GRID_W = 64
D_SSM = 1024
GROUP_CH = 16
SSM_GROUPS = D_SSM // GROUP_CH
SSM_STATE = 64
N_HEADS = 8
HEAD_DIM = 64
V_DIM = 2 * HEAD_DIM
D_QK = N_HEADS * 2 * HEAD_DIM
D_ATTN = N_HEADS * V_DIM
ATTN_SCALE = HEAD_DIM ** -0.5
Q_SCALE = ATTN_SCALE * math.log2(math.e)
ROPE_AXIS_DIM = HEAD_DIM // 2
ROPE_BASE = 10000.0
D_FF = 5632
D_IN = D_SSM + 2 * D_QK + D_ATTN + 2 * D_MODEL
EPS = 1e-6
LAM_INIT = 0.8 - 0.6 * math.exp(-0.3 * 0)

LANES = 128
SUBLANES = 8
MXU_DIM = 256
SSM_CHUNK = 16
CHUNK_W = SSM_CHUNK * GROUP_CH
STATE_W = 2 * SSM_STATE
VMEM_LIMIT = 56 * 1024 * 1024


def _cparams(*sem):
    return pltpu.CompilerParams(dimension_semantics=sem, vmem_limit_bytes=VMEM_LIMIT)


def _dot(a, b):
    return jnp.dot(a, b, preferred_element_type=F32)


def _dot_nt(a, b, precision=None):
    return lax.dot_general(a, b, (((1,), (1,)), ((), ())), preferred_element_type=F32,
                           precision=precision)


def _dot_tn(a, b):
    return lax.dot_general(a, b, (((0,), (0,)), ((), ())), preferred_element_type=F32)


def _sigmoid(x):
    return 1.0 / (1.0 + jnp.exp(-x))


def _col_tiles(w, tn):
    k, n = w.shape
    return jnp.transpose(w.reshape(k, n // tn, tn), (1, 0, 2)).astype(BF16)


def _adaln_kernel(c_ref, w_ref, b_ref, o_ref):
    c = c_ref[...]
    s = (c * _sigmoid(c)).astype(BF16)
    o_ref[...] = _dot(s, w_ref[...].astype(BF16)) + b_ref[...]


def _adaln(cc, w_ada, b_ada):
    rows, d = cc.shape
    n = w_ada.shape[1]
    tn = 1024
    return pl.pallas_call(
        _adaln_kernel,
        grid=(n // tn,),
        in_specs=[pl.BlockSpec((rows, d), lambda j: (0, 0)),
                  pl.BlockSpec((d, tn), lambda j: (0, j)),
                  pl.BlockSpec((1, tn), lambda j: (0, j))],
        out_specs=pl.BlockSpec((rows, tn), lambda j: (0, j)),
        out_shape=jax.ShapeDtypeStruct((rows, n), F32),
        compiler_params=_cparams("arbitrary"),
        name="adaln",
    )(cc, w_ada, b_ada.reshape(1, n))


def _norm_mod(x, g, shift, scale):
    ms = jnp.mean(x * x, axis=-1, keepdims=True)
    y = x * lax.rsqrt(ms + EPS) * g
    return y * (1.0 + scale) + shift


def _prenorm_kernel(x_ref, mod_ref, g_ref, h_ref):
    h_ref[...] = _norm_mod(x_ref[...], g_ref[...], mod_ref[0:1, :], mod_ref[1:2, :]).astype(BF16)


def _prenorm(x2d, mod, norm_g, n_tok):
    t = x2d.shape[0]
    tm = min(512, n_tok)
    tiles_per_b = n_tok // tm
    per_batch = mod.shape[0] > 1
    return pl.pallas_call(
        _prenorm_kernel,
        grid=(t // tm,),
        in_specs=[pl.BlockSpec((tm, D_MODEL), lambda i: (i, 0)),
                  pl.BlockSpec((None, 6, D_MODEL),
                               (lambda i: (i // tiles_per_b, 0, 0)) if per_batch else (lambda i: (0, 0, 0))),
                  pl.BlockSpec((1, D_MODEL), lambda i: (0, 0))],
        out_specs=pl.BlockSpec((tm, D_MODEL), lambda i: (i, 0)),
        out_shape=jax.ShapeDtypeStruct((t, D_MODEL), BF16),
        compiler_params=_cparams("arbitrary"),
        name="prenorm",
    )(x2d, mod, norm_g)


PROJ_TN = 1024


def _rope_tables(n_lat):
    pos = jnp.arange(n_lat, dtype=jnp.int32)
    row = (pos // GRID_W).astype(F32)
    col = (pos % GRID_W).astype(F32)
    inv = 1.0 / (ROPE_BASE ** (jnp.arange(0, ROPE_AXIS_DIM, 2, dtype=F32) / ROPE_AXIS_DIM))
    lane = jnp.arange(LANES)
    d = lane % HEAD_DIM
    freq = inv[d % (ROPE_AXIS_DIM // 2)]
    ang = jnp.where(d < ROPE_AXIS_DIM, row[:, None], col[:, None]) * freq[None, :]
    first = (lane % ROPE_AXIS_DIM) < (ROPE_AXIS_DIM // 2)
    cos, sin = jnp.cos(ang), jnp.sin(ang)
    return cos, jnp.where(first, -sin, 0.0), jnp.where(first, 0.0, sin)


def _proj_rope_kernel(h_ref, w_ref, cos_ref, sa_ref, sb_ref, o_ref):
    scale = jnp.where(pl.program_id(1) == 0, Q_SCALE, 1.0).astype(F32)
    cos = cos_ref[...] * scale
    sa = sa_ref[...] * scale
    sb = sb_ref[...] * scale
    for c in range(PROJ_TN // MXU_DIM):
        acc = _dot(h_ref[...], w_ref[:, c * MXU_DIM:(c + 1) * MXU_DIM])
        for hh in range(MXU_DIM // LANES):
            a = acc[:, hh * LANES:(hh + 1) * LANES]
            r = a * cos + pltpu.roll(a, LANES - 16, 1) * sa + pltpu.roll(a, 16, 1) * sb
            o_ref[c * (MXU_DIM // LANES) + hh] = r.astype(BF16)


def _proj_heads_kernel(h_ref, w_ref, o_ref):
    for c in range(PROJ_TN // MXU_DIM):
        acc = _dot(h_ref[...], w_ref[:, c * MXU_DIM:(c + 1) * MXU_DIM])
        for hh in range(MXU_DIM // LANES):
            o_ref[c * (MXU_DIM // LANES) + hh] = acc[:, hh * LANES:(hh + 1) * LANES].astype(BF16)


def _proj_gate_kernel(h_ref, w_ref, o_ref):
    for c in range(PROJ_TN // MXU_DIM):
        acc = _dot(h_ref[...], w_ref[:, c * MXU_DIM:(c + 1) * MXU_DIM])
        o_ref[:, c * MXU_DIM:(c + 1) * MXU_DIM] = _sigmoid(acc).astype(BF16)


def _proj_tm(t):
    return min(1024, t)


def _proj_rope(h, w_in, n_lat):
    t = h.shape[0]
    tm = min(_proj_tm(t), n_lat)
    tiles_per_b = n_lat // tm
    tab = pl.BlockSpec((tm, LANES), lambda i, j: (i % tiles_per_b, 0))
    col0 = D_SSM // PROJ_TN
    return pl.pallas_call(
        _proj_rope_kernel,
        grid=(t // tm, 2),
        in_specs=[pl.BlockSpec((tm, D_MODEL), lambda i, j: (i, 0)),
                  pl.BlockSpec((None, D_MODEL, PROJ_TN), lambda i, j: (col0 + j, 0, 0)),
                  tab, tab, tab],
        out_specs=pl.BlockSpec((None, N_HEADS, tm, LANES), lambda i, j: (j, 0, i, 0)),
        out_shape=jax.ShapeDtypeStruct((2, N_HEADS, t, LANES), BF16),
        compiler_params=_cparams("arbitrary", "arbitrary"),
        name="proj_qk",
    )(h, w_in, *_rope_tables(n_lat))


def _proj_heads(h, w_in, col0, n_out):
    t = h.shape[0]
    tm = _proj_tm(t)
    return pl.pallas_call(
        _proj_heads_kernel,
        grid=(t // tm, n_out),
        in_specs=[pl.BlockSpec((tm, D_MODEL), lambda i, j: (i, 0)),
                  pl.BlockSpec((None, D_MODEL, PROJ_TN), lambda i, j: (col0 + j, 0, 0))],
        out_specs=pl.BlockSpec((None, N_HEADS, tm, LANES), lambda i, j: (j, 0, i, 0)),
        out_shape=jax.ShapeDtypeStruct((n_out, N_HEADS, t, LANES), BF16),
        compiler_params=_cparams("arbitrary", "arbitrary"),
        name="proj_heads",
    )(h, w_in)


def _proj_gates(h, w_in):
    t = h.shape[0]
    tm = _proj_tm(t)
    col0 = (D_SSM + 2 * D_QK + D_ATTN) // PROJ_TN
    return pl.pallas_call(
        _proj_gate_kernel,
        grid=(t // tm, 2 * D_MODEL // PROJ_TN),
        in_specs=[pl.BlockSpec((tm, D_MODEL), lambda i, j: (i, 0)),
                  pl.BlockSpec((None, D_MODEL, PROJ_TN), lambda i, j: (col0 + j, 0, 0))],
        out_specs=pl.BlockSpec((tm, PROJ_TN), lambda i, j: (i, j)),
        out_shape=jax.ShapeDtypeStruct((t, 2 * D_MODEL), BF16),
        compiler_params=_cparams("arbitrary", "arbitrary"),
        name="proj_gates",
    )(h, w_in)


U_ROWS = 512
GROUPS_PER_TILE = LANES // GROUP_CH


def _proj_u_kernel(h_ref, w_ref, z_ref, u_ref):
    n_tok = h_ref.shape[0]
    nch = n_tok // SSM_CHUNK
    for r in range(n_tok // U_ROWS):
        rows = slice(r * U_ROWS, (r + 1) * U_ROWS)
        u = _dot(h_ref[rows, :], w_ref[...])
        for lt in range(D_SSM // LANES):
            u_ref[lt, rows, :] = u[:, lt * LANES:(lt + 1) * LANES]
    for t in range(SSM_CHUNK):
        for lt in range(D_SSM // LANES):
            comb = u_ref[lt, pl.ds(t, nch, stride=SSM_CHUNK), :]
            z_ref[lt * GROUPS_PER_TILE:(lt + 1) * GROUPS_PER_TILE, t] = (
                comb.T.reshape(GROUPS_PER_TILE, GROUP_CH, nch).astype(BF16))


def _proj_u(h, w_u):
    t = h.shape[0]
    tm = min(SSM_CHUNK * LANES, t)
    nch = tm // SSM_CHUNK
    return pl.pallas_call(
        _proj_u_kernel,
        grid=(t // tm,),
        in_specs=[pl.BlockSpec((tm, D_MODEL), lambda i: (i, 0)),
                  pl.BlockSpec((None, D_MODEL, D_SSM), lambda i: (0, 0, 0),
                               pipeline_mode=pl.Buffered(1))],
        out_specs=pl.BlockSpec((SSM_GROUPS, SSM_CHUNK, GROUP_CH, nch), lambda i: (0, 0, 0, i)),
        out_shape=jax.ShapeDtypeStruct((SSM_GROUPS, SSM_CHUNK, GROUP_CH, t // SSM_CHUNK), BF16),
        scratch_shapes=[pltpu.VMEM((D_SSM // LANES, tm, LANES), F32)],
        compiler_params=_cparams("arbitrary"),
        name="proj_u",
    )(h, w_u)


def _ssm_param_kernel(a_re_ref, a_im_ref, ldt_ref, bt_ref, btsw_ref, cr_ref, ci_ref, d_ref,
                      m_ref, w_ref, v_ref, a_ref):
    tc = SSM_CHUNK
    lane = lax.broadcasted_iota(jnp.int32, (1, STATE_W), 1)
    lo = lane < SSM_STATE
    kts = []
    for d in range(2):
        dt = jnp.exp(ldt_ref[d:d + 1, :])
        lre = a_re_ref[d:d + 1, :] * dt
        lim = a_im_ref[d:d + 1, :] * dt
        kk = lax.broadcasted_iota(jnp.int32, (tc + 1, STATE_W), 0).astype(F32)
        mag = jnp.exp(kk * lre)
        c2 = mag * jnp.cos(kk * lim)
        s2 = mag * jnp.sin(kk * lim)
        xr = c2[1:2] - 1.0
        xi = s2[1:2]
        ar = a_re_ref[d:d + 1, :]
        ai = a_im_ref[d:d + 1, :]
        den = ar * ar + ai * ai
        er = (xr * ar + xi * ai) / den
        ei = (xi * ar - xr * ai) / den
        e2 = jnp.where(lo, -ei, ei)
        bt = bt_ref[d]
        btsw = btsw_ref[d]
        bb = er * bt + e2 * btsw
        bbsw = er * btsw - e2 * bt
        cdr = cr_ref[d]
        cdi = ci_ref[d]
        cneg = jnp.where(lo, cdr, -cdi)
        p2 = jnp.where(lo, -s2, s2)
        q1 = jnp.where(lo, c2, -s2)
        q2 = jnp.where(lo, -s2, -c2)
        gl = []
        for s in range(tc):
            kw = tc - 1 - s if d == 0 else s
            wd = c2[kw:kw + 1] * bb + p2[kw:kw + 1] * bbsw
            wsw = c2[kw:kw + 1] * bbsw - p2[kw:kw + 1] * bb
            w_ref[s, :, (2 * d) * STATE_W:(2 * d + 1) * STATE_W] = wd.astype(w_ref.dtype)
            w_ref[s, :, (2 * d + 1) * STATE_W:(2 * d + 2) * STATE_W] = wsw.astype(w_ref.dtype)
            kv = s + 1 if d == 0 else tc - s
            v_ref[s, :, d * STATE_W:(d + 1) * STATE_W] = (
                cdr * q1[kv:kv + 1] + cdi * q2[kv:kv + 1]).astype(v_ref.dtype)
            kl = tc - 1 - s if d == 0 else s
            gl.append(c2[kl:kl + 1] * bb + p2[kl:kl + 1] * bbsw)
        gcat = jnp.concatenate(gl, axis=0)
        kts.append(_dot_nt(cneg, gcat, precision=lax.Precision.HIGHEST))
        a_ref[d, 0:1, :] = c2[tc:tc + 1]
        a_ref[d, 1:2, :] = p2[tc:tc + 1]
    lane_w = lax.broadcasted_iota(jnp.int32, (GROUP_CH, CHUNK_W), 1)
    row_w = lax.broadcasted_iota(jnp.int32, (GROUP_CH, CHUNK_W), 0)
    dcol = d_ref[...]
    for t in range(tc):
        sh = (CHUNK_W - GROUP_CH * (tc - 1 - t)) % CHUNK_W
        fwd = kts[0] if sh == 0 else pltpu.roll(kts[0], sh, 1)
        fwd = jnp.where(lane_w < GROUP_CH * (t + 1), fwd, 0.0)
        rev = kts[1] if t == 0 else pltpu.roll(kts[1], GROUP_CH * t, 1)
        rev = jnp.where(lane_w >= GROUP_CH * t, rev, 0.0)
        skip = jnp.where(lane_w == GROUP_CH * t + row_w, dcol, 0.0)
        m_ref[t] = (fwd + rev + skip).astype(m_ref.dtype)


def _ssm_params(a_re, a_im, log_dt, b_re, b_im, c_re, c_im, d_skip):
    g = SSM_GROUPS
    dup = lambda a: jnp.concatenate([a, a], axis=-1)
    a_re2 = dup(jnp.transpose(a_re, (1, 0, 2)))
    a_im2 = dup(jnp.transpose(a_im, (1, 0, 2)))
    ldt = jnp.transpose(log_dt, (1, 0))[:, :, None]
    btr = jnp.transpose(b_re, (1, 0, 3, 2))
    bti = jnp.transpose(b_im, (1, 0, 3, 2))
    bt = jnp.concatenate([btr, bti], axis=-1)
    btsw = jnp.concatenate([bti, btr], axis=-1)
    cdr = dup(jnp.transpose(c_re, (1, 0, 2, 3)))
    cdi = dup(jnp.transpose(c_im, (1, 0, 2, 3)))
    dcol = d_skip.reshape(g, GROUP_CH, 1)
    g3 = lambda n: pl.BlockSpec((None, 2, n), lambda i: (i, 0, 0))
    g4 = lambda r, n: pl.BlockSpec((None, 2, r, n), lambda i: (i, 0, 0, 0))
    tc = SSM_CHUNK
    out = lambda n: pl.BlockSpec((None, tc, GROUP_CH, n), lambda i: (i, 0, 0, 0))
    m, w, v, a = pl.pallas_call(
        _ssm_param_kernel,
        grid=(g,),
        in_specs=[g3(STATE_W), g3(STATE_W), g3(1),
                  g4(GROUP_CH, STATE_W), g4(GROUP_CH, STATE_W),
                  g4(GROUP_CH, STATE_W), g4(GROUP_CH, STATE_W),
                  pl.BlockSpec((None, GROUP_CH, 1), lambda i: (i, 0, 0))],
        out_specs=[out(CHUNK_W), out(4 * STATE_W), out(2 * STATE_W),
                   pl.BlockSpec((None, 2, 2, STATE_W), lambda i: (i, 0, 0, 0))],
        out_shape=[jax.ShapeDtypeStruct((g, tc, GROUP_CH, CHUNK_W), BF16),
                   jax.ShapeDtypeStruct((g, tc, GROUP_CH, 4 * STATE_W), BF16),
                   jax.ShapeDtypeStruct((g, tc, GROUP_CH, 2 * STATE_W), BF16),
                   jax.ShapeDtypeStruct((g, 2, 2, STATE_W), F32)],
        compiler_params=_cparams("arbitrary"),
        name="ssm_params",
    )(a_re2, a_im2, ldt, bt, btsw, cdr, cdi, dcol)
    return (m.reshape(g, CHUNK_W, CHUNK_W), w.reshape(g, CHUNK_W, 4 * STATE_W),
            v.reshape(g, CHUNK_W, 2 * STATE_W), a)


ROW_PAD = SUBLANES


def _ssm_kernel(bsz, zl_ref, zc_ref, m_ref, w_ref, v_ref, a_ref, y_ref, sl_ref, sc_ref, h_ref):
    ncl = zl_ref.shape[-1] // bsz
    ncc = zc_ref.shape[-1] // bsz
    pl_l = ncl + ROW_PAD
    pl_c = ncc + ROW_PAD
    zl = zl_ref[...].reshape(CHUNK_W, bsz * ncl)
    zc = zc_ref[...].reshape(CHUNK_W, bsz * ncc)
    w = w_ref[...]
    s_c = _dot_tn(zc, w)
    for b in range(bsz):
        s_l = _dot_tn(zl[:, b * ncl:(b + 1) * ncl], w)
        for k in range(4):
            sl_ref[k, b * pl_l:b * pl_l + ncl, :] = s_l[:, k * STATE_W:(k + 1) * STATE_W]
            sc_ref[k, b * pl_c:b * pl_c + ncc, :] = s_c[b * ncc:(b + 1) * ncc,
                                                        k * STATE_W:(k + 1) * STATE_W]

    a1 = (a_ref[0, 0:1, :], a_ref[1, 0:1, :])
    a2 = (a_ref[0, 1:2, :], a_ref[1, 1:2, :])

    def step(d, h, hsw, s, ssw):
        return a1[d] * h + a2[d] * hsw + s, a1[d] * hsw - a2[d] * h + ssw

    def ctx_body(j, hs):
        hf, hfs, hr, hrs = hs
        rf = pl.ds(j, bsz, stride=pl_c)
        rr = pl.ds(ncc - 1 - j, bsz, stride=pl_c)
        hf, hfs = step(0, hf, hfs, sc_ref[0, rf, :], sc_ref[1, rf, :])
        hr, hrs = step(1, hr, hrs, sc_ref[2, rr, :], sc_ref[3, rr, :])
        return hf, hfs, hr, hrs

    zero = jnp.zeros((bsz, STATE_W), F32)
    hs = lax.fori_loop(0, ncc, ctx_body, (zero, zero, zero, zero))

    def lat_body(j, hs):
        hf, hfs, hr, hrs = hs
        rf = pl.ds(j, bsz, stride=pl_l)
        rr = pl.ds(ncl - 1 - j, bsz, stride=pl_l)
        h_ref[0, rf, :] = hf
        h_ref[1, rr, :] = hr
        hf, hfs = step(0, hf, hfs, sl_ref[0, rf, :], sl_ref[1, rf, :])
        hr, hrs = step(1, hr, hrs, sl_ref[2, rr, :], sl_ref[3, rr, :])
        return hf, hfs, hr, hrs

    lax.fori_loop(0, ncl, lat_body, hs)
    hcat = jnp.concatenate(
        [jnp.concatenate([h_ref[0, b * pl_l:b * pl_l + ncl, :], h_ref[1, b * pl_l:b * pl_l + ncl, :]],
                         axis=1) for b in range(bsz)], axis=0).astype(BF16)
    y = _dot(m_ref[...], zl) + _dot_nt(v_ref[...], hcat)
    y_ref[...] = y.reshape(SSM_CHUNK, GROUP_CH, bsz * ncl).astype(y_ref.dtype)


def _ssm_scan(zl, zc, m, w, v, a, bsz):
    g = zl.shape[0]
    nl = zl.shape[-1]
    nc = zc.shape[-1]
    blk = lambda *s: pl.BlockSpec((None,) + s, lambda i: (i,) + (0,) * len(s))
    return pl.pallas_call(
        functools.partial(_ssm_kernel, bsz),
        grid=(g,),
        in_specs=[blk(SSM_CHUNK, GROUP_CH, nl), blk(SSM_CHUNK, GROUP_CH, nc), blk(CHUNK_W, CHUNK_W),
                  blk(CHUNK_W, 4 * STATE_W), blk(CHUNK_W, 2 * STATE_W), blk(2, 2, STATE_W)],
        out_specs=blk(SSM_CHUNK, GROUP_CH, nl),
        out_shape=jax.ShapeDtypeStruct((g, SSM_CHUNK, GROUP_CH, nl), BF16),
        scratch_shapes=[pltpu.VMEM((4, nl + bsz * ROW_PAD, STATE_W), F32),
                        pltpu.VMEM((4, nc + bsz * ROW_PAD, STATE_W), F32),
                        pltpu.VMEM((2, nl + bsz * ROW_PAD, STATE_W), F32)],
        compiler_params=_cparams("arbitrary"),
        name="ssm_scan",
    )(zl, zc, m, w, v, a)


ATTN_SUB = 64


def _attn_kernel(lam_ref, g_ref, q_ref, kl_ref, vl_ref, kc_ref, vc_ref, o_ref, k_scr, v_scr):
    tq = q_ref.shape[0]
    n_lat = kl_ref.shape[0]
    n_ctx = kc_ref.shape[0]

    @pl.when(pl.program_id(2) == 0)
    def _():
        k_scr[0:n_lat] = kl_ref[...]
        k_scr[n_lat:n_lat + n_ctx] = kc_ref[...]
        v_scr[0:n_lat, 0:V_DIM] = vl_ref[...]
        v_scr[n_lat:n_lat + n_ctx, 0:V_DIM] = vc_ref[...]
        v_scr[:, V_DIM:2 * V_DIM] = jnp.ones((n_lat + n_ctx, V_DIM), BF16)

    lp = lam_ref[...]
    lam = (jnp.exp(jnp.sum(lp[0:1] * lp[1:2], axis=1, keepdims=True))
           - jnp.exp(jnp.sum(lp[2:3] * lp[3:4], axis=1, keepdims=True)) + LAM_INIT)
    gain = g_ref[...] * (1.0 - LAM_INIT)
    lane = lax.broadcasted_iota(jnp.int32, (ATTN_SUB, LANES), 1)
    zero = jnp.zeros((ATTN_SUB, LANES), BF16)
    for c in range(tq // ATTN_SUB):
        q = q_ref[c * ATTN_SUB:(c + 1) * ATTN_SUB, :]
        qq = jnp.concatenate([jnp.where(lane < HEAD_DIM, q, zero),
                              jnp.where(lane >= HEAD_DIM, q, zero)], axis=0)
        s = _dot_nt(qq, k_scr[...])
        mx = jnp.max(s, axis=-1, keepdims=True)
        p = jnp.exp2((s - mx).astype(BF16))
        oe = _dot(p, v_scr[...])
        o2 = oe[:, 0:V_DIM] / oe[:, V_DIM:2 * V_DIM]
        o = o2[0:ATTN_SUB] - lam * o2[ATTN_SUB:2 * ATTN_SUB]
        o = o * lax.rsqrt(jnp.mean(o * o, axis=-1, keepdims=True) + EPS) * gain
        o_ref[c * ATTN_SUB:(c + 1) * ATTN_SUB, :] = o.astype(o_ref.dtype)


def _attention(lam_p, subln_g, qk, v, kvc, bsz, n_lat, n_ctx):
    tq = min(1024, n_lat)
    nq = n_lat // tq
    t = bsz * n_lat
    sel = lambda which, rows: pl.BlockSpec((None, None, rows, LANES),
                                           lambda b, h, i: (which, h, b, 0))
    return pl.pallas_call(
        _attn_kernel,
        grid=(bsz, N_HEADS, nq),
        scratch_shapes=[pltpu.VMEM((n_lat + n_ctx, LANES), BF16),
                        pltpu.VMEM((n_lat + n_ctx, 2 * V_DIM), BF16)],
        in_specs=[pl.BlockSpec((4, HEAD_DIM), lambda b, h, i: (0, 0)),
                  pl.BlockSpec((1, V_DIM), lambda b, h, i: (0, 0)),
                  pl.BlockSpec((None, None, tq, LANES), lambda b, h, i: (0, h, b * nq + i, 0)),
                  sel(1, n_lat), sel(0, n_lat), sel(0, n_ctx), sel(1, n_ctx)],
        out_specs=pl.BlockSpec((None, tq, LANES), lambda b, h, i: (h, b * nq + i, 0)),
        out_shape=jax.ShapeDtypeStruct((N_HEADS, t, LANES), BF16),
        compiler_params=_cparams("arbitrary", "arbitrary", "arbitrary"),
        name="diff_attn",
    )(lam_p, subln_g, qk, qk, v, kvc, kvc)


def _unchunk_kernel(y_ref, o_ref, nat_ref):
    nch = y_ref.shape[-1]
    for t in range(SSM_CHUNK):
        yt = y_ref[:, t].reshape(D_SSM, nch).astype(F32).T
        for lt in range(D_SSM // LANES):
            nat_ref[lt, pl.ds(t, nch, stride=SSM_CHUNK), :] = yt[:, lt * LANES:(lt + 1) * LANES]
    for lt in range(D_SSM // LANES):
        o_ref[:, lt * LANES:(lt + 1) * LANES] = nat_ref[lt].astype(o_ref.dtype)


def _unchunk(yt, bsz, n_lat):
    nch = n_lat // SSM_CHUNK
    return pl.pallas_call(
        _unchunk_kernel,
        grid=(bsz,),
        in_specs=[pl.BlockSpec((SSM_GROUPS, SSM_CHUNK, GROUP_CH, nch), lambda b: (0, 0, 0, b))],
        out_specs=pl.BlockSpec((n_lat, D_SSM), lambda b: (b, 0)),
        out_shape=jax.ShapeDtypeStruct((bsz * n_lat, D_SSM), BF16),
        scratch_shapes=[pltpu.VMEM((D_SSM // LANES, n_lat, LANES), F32)],
        compiler_params=_cparams("arbitrary"),
        name="unchunk",
    )(yt)


MERGE_SUB = 256


def _merge_kernel(y_ref, o_ref, g_ref, x_ref, mod_ref, wg_ref, bg_ref, wps_ref, wpa_ref, wo_ref,
                  out_ref):
    for r in range(x_ref.shape[0] // MERGE_SUB):
        rows = slice(r * MERGE_SUB, (r + 1) * MERGE_SUB)
        y = y_ref[rows, :].astype(F32)
        z = 0.5 * y * (1.0 + jnp.tanh(math.sqrt(2.0 / math.pi) * (y + 0.044715 * (y * y * y))))
        gl = _dot(z.astype(BF16), wg_ref[...]) + bg_ref[...]
        ys = (z * _sigmoid(gl)).astype(BF16)
        oa = jnp.concatenate([o_ref[h, rows, :] for h in range(N_HEADS)], axis=1)
        merged = (g_ref[rows, 0:D_MODEL].astype(F32) * _dot(ys, wps_ref[...])
                  + g_ref[rows, D_MODEL:2 * D_MODEL].astype(F32) * _dot(oa, wpa_ref[...]))
        out = _dot(merged.astype(BF16), wo_ref[...])
        out_ref[rows, :] = x_ref[rows, :] + mod_ref[2:3, :] * out


def _merge(y, o, gates, x2d, mod, w_glu, b_glu, w_ps, w_pa, w_o, n_lat):
    t = x2d.shape[0]
    tm = min(512, n_lat)
    tiles_per_b = n_lat // tm
    row = lambda i: (i, 0)
    const = lambda shape: pl.BlockSpec(shape, lambda i: (0,) * len(shape),
                                       pipeline_mode=pl.Buffered(1))
    return pl.pallas_call(
        _merge_kernel,
        grid=(t // tm,),
        in_specs=[pl.BlockSpec((tm, D_SSM), row),
                  pl.BlockSpec((N_HEADS, tm, LANES), lambda i: (0, i, 0)),
                  pl.BlockSpec((tm, 2 * D_MODEL), row),
                  pl.BlockSpec((tm, D_MODEL), row),
                  pl.BlockSpec((None, 6, D_MODEL), lambda i: (i // tiles_per_b, 0, 0)),
                  const((D_SSM, D_SSM)), const((1, D_SSM)),
                  const((D_SSM, D_MODEL)), const((D_ATTN, D_MODEL)), const((D_MODEL, D_MODEL))],
        out_specs=pl.BlockSpec((tm, D_MODEL), row),
        out_shape=jax.ShapeDtypeStruct((t, D_MODEL), F32),
        compiler_params=_cparams("arbitrary"),
        name="merge",
    )(y, o, gates, x2d, mod, w_glu, b_glu, w_ps, w_pa, w_o)


FFN_SUB = 256


def _ffn_kernel(x_ref, mod_ref, g2_ref, gf_ref, wg_ref, wu_ref, wo_ref, out_ref, h_ref):
    j = pl.program_id(1)
    last = pl.num_programs(1) - 1

    def chains(first, final):
        for r in range(x_ref.shape[0] // FFN_SUB):
            rows = slice(r * FFN_SUB, (r + 1) * FFN_SUB)
            if first:
                h_ref[rows, :] = _norm_mod(x_ref[rows, :], g2_ref[...], mod_ref[3:4, :],
                                           mod_ref[4:5, :]).astype(BF16)
            h = h_ref[rows, :]
            gate = _dot(h, wg_ref[...])
            up = _dot(h, wu_ref[...])
            act = (gate * _sigmoid(gate) * up).astype(BF16)
            part = _dot(act, wo_ref[...])
            if first:
                out_ref[rows, :] = part
            elif not final:
                out_ref[rows, :] += part
            else:
                xo = x_ref[rows, :] + mod_ref[5:6, :] * (out_ref[rows, :] + part)
                ms = jnp.mean(xo * xo, axis=-1, keepdims=True)
                out_ref[rows, :] = xo * lax.rsqrt(ms + EPS) * gf_ref[...]

    pl.when(j == 0)(lambda: chains(True, False))
    pl.when((j > 0) & (j < last))(lambda: chains(False, False))
    pl.when(j == last)(lambda: chains(False, True))


FFN_TF = 512


def _ffn(x_mid, mod, norm2_g, norm_f_g, w_ffn_in, w_ffn_out, n_lat):
    t = x_mid.shape[0]
    tm = min(512, n_lat)
    tf = FFN_TF
    nf = D_FF // tf
    assert nf >= 2, "the kernel needs distinct first and last hidden-dimension steps"
    tiles_per_b = n_lat // tm
    row = lambda i, j: (i, 0)
    return pl.pallas_call(
        _ffn_kernel,
        grid=(t // tm, nf),
        in_specs=[pl.BlockSpec((tm, D_MODEL), row),
                  pl.BlockSpec((None, 6, D_MODEL), lambda i, j: (i // tiles_per_b, 0, 0)),
                  pl.BlockSpec((1, D_MODEL), lambda i, j: (0, 0)),
                  pl.BlockSpec((1, D_MODEL), lambda i, j: (0, 0)),
                  pl.BlockSpec((None, D_MODEL, tf), lambda i, j: (j, 0, 0)),
                  pl.BlockSpec((None, D_MODEL, tf), lambda i, j: (j + nf, 0, 0)),
                  pl.BlockSpec((tf, D_MODEL), lambda i, j: (j, 0))],
        out_specs=pl.BlockSpec((tm, D_MODEL), row),
        out_shape=jax.ShapeDtypeStruct((t, D_MODEL), F32),
        scratch_shapes=[pltpu.VMEM((tm, D_MODEL), BF16)],
        compiler_params=_cparams("arbitrary", "arbitrary"),
        name="ffn",
    )(x_mid, mod, norm2_g, norm_f_g, w_ffn_in, w_ffn_in, w_ffn_out)


def kernel(x, c, ctx, c_ctx, w_ada, b_ada, norm1_g, w_in, ssm_a_re, ssm_a_im, ssm_log_dt, ssm_b_re, ssm_b_im, ssm_c_re, ssm_c_im, ssm_d, w_glu, b_glu, lambda_q1, lambda_k1, lambda_q2, lambda_k2, subln_g, w_proj_ssm, w_proj_attn, w_out, norm2_g, w_ffn_in, w_ffn_out, norm_f_g):
    bsz, n_lat, d = x.shape
    n_ctx = ctx.shape[1]
    assert d == D_MODEL and w_ada.shape[0] == 1, "single-layer block only"
    assert n_lat % GRID_W == 0 and n_lat % SSM_CHUNK == 0 and n_ctx % SSM_CHUNK == 0
    assert bsz % SUBLANES == 0

    pad = (-(bsz + 1)) % SUBLANES
    cc = jnp.concatenate([c, c_ctx[None, :], jnp.zeros((pad, d), F32)], axis=0)
    mod_all = _adaln(cc, w_ada[0], b_ada[0])
    mod = mod_all[:bsz].reshape(bsz, 6, d)
    mod_c = mod_all[bsz:bsz + 1].reshape(1, 6, d)

    w_in_b = _col_tiles(w_in[0], PROJ_TN)
    g1 = norm1_g[0].reshape(1, d)
    x2d = x.reshape(bsz * n_lat, d)
    h = _prenorm(x2d, mod, g1, n_lat)
    hc = _prenorm(ctx.reshape(bsz * n_ctx, d), mod_c, g1, n_ctx)

    qk = _proj_rope(h, w_in_b, n_lat)
    v = _proj_heads(h, w_in_b, (D_SSM + 2 * D_QK) // PROJ_TN, 1)
    kvc = _proj_heads(hc, w_in_b, (D_SSM + D_QK) // PROJ_TN, 2)
    gates = _proj_gates(h, w_in_b)
    zl = _proj_u(h, w_in_b)
    zc = _proj_u(hc, w_in_b)

    m, w, vv, a = _ssm_params(ssm_a_re[0], ssm_a_im[0], ssm_log_dt[0], ssm_b_re[0], ssm_b_im[0],
                              ssm_c_re[0], ssm_c_im[0], ssm_d[0])
    yt = _ssm_scan(zl, zc, m, w, vv, a, bsz)

    lam_p = jnp.stack([lambda_q1[0], lambda_k1[0], lambda_q2[0], lambda_k2[0]]).astype(F32)
    o = _attention(lam_p, subln_g[0].reshape(1, V_DIM), qk, v, kvc, bsz, n_lat, n_ctx)

    y = _unchunk(yt, bsz, n_lat)
    x_mid = _merge(y, o, gates, x2d, mod, w_glu[0].astype(BF16), b_glu[0].reshape(1, D_SSM),
                   w_proj_ssm[0].astype(BF16), w_proj_attn[0].astype(BF16), w_out[0].astype(BF16),
                   n_lat)
    out = _ffn(x_mid, mod, norm2_g[0].reshape(1, d), norm_f_g.reshape(1, d),
               _col_tiles(w_ffn_in[0], FFN_TF), w_ffn_out[0].astype(BF16), n_lat)
    return out.reshape(bsz, n_lat, d)
```

```python
import functools
import math

import jax
import jax.numpy as jnp
from jax import lax
from jax.experimental import pallas as pl
from jax.experimental.pallas import tpu as pltpu

F32 = jnp.float32
BF16 = jnp.bfloat16

D_MODEL = 2048
GRID_W = 64
D_SSM = 1024
GROUP_CH = 16
SSM_GROUPS = D_SSM // GROUP_CH
SSM_STATE = 64
N_HEADS = 8
HEAD_DIM = 64
V_DIM = 2 * HEAD_DIM
D_QK = N_HEADS * 2 * HEAD_DIM
D_ATTN = N_HEADS * V_DIM
ATTN_SCALE = HEAD_DIM ** -0.5
Q_SCALE = ATTN_SCALE * math.log2(math.e)
ROPE_AXIS_DIM = HEAD_DIM // 2
ROPE_BASE = 10000.0
D_FF = 5632
D_IN = D_SSM + 2 * D_QK + D_ATTN + 2 * D_MODEL
EPS = 1e-6
LAM_INIT = 0.8 - 0.6 * math.exp(-0.3 * 0)

LANES = 128
SUBLANES = 8
MXU_DIM = 256
SSM_CHUNK = 16
CHUNK_W = SSM_CHUNK * GROUP_CH
STATE_W = 2 * SSM_STATE
VMEM_LIMIT = 56 * 1024 * 1024


def _cparams(*sem):
    return pltpu.CompilerParams(dimension_semantics=sem, vmem_limit_bytes=VMEM_LIMIT)


def _dot(a, b):
    return jnp.dot(a, b, preferred_element_type=F32)


def _dot_nt(a, b, precision=None):
    return lax.dot_general(a, b, (((1,), (1,)), ((), ())), preferred_element_type=F32,
                           precision=precision)


def _dot_tn(a, b):
    return lax.dot_general(a, b, (((0,), (0,)), ((), ())), preferred_element_type=F32)


def _sigmoid(x):
    return 1.0 / (1.0 + jnp.exp(-x))


def _adaln_kernel(c_ref, w_ref, b_ref, o_ref):
    c = c_ref[...]
    s = (c * _sigmoid(c)).astype(BF16)
    o_ref[...] = _dot(s, w_ref[...].astype(BF16)) + b_ref[...]


def _adaln(cc, w_ada, b_ada):
    rows, d = cc.shape
    n = w_ada.shape[1]
    tn = 1024
    return pl.pallas_call(
        _adaln_kernel,
        grid=(n // tn,),
        in_specs=[pl.BlockSpec((rows, d), lambda j: (0, 0)),
                  pl.BlockSpec((d, tn), lambda j: (0, j)),
                  pl.BlockSpec((1, tn), lambda j: (0, j))],
        out_specs=pl.BlockSpec((rows, tn), lambda j: (0, j)),
        out_shape=jax.ShapeDtypeStruct((rows, n), F32),
        compiler_params=_cparams("arbitrary"),
        name="adaln",
    )(cc, w_ada, b_ada.reshape(1, n))


def _norm_mod(x, g, shift, scale):
    ms = jnp.mean(x * x, axis=-1, keepdims=True)
    y = x * lax.rsqrt(ms + EPS) * g
    return y * (1.0 + scale) + shift


def _prenorm_kernel(x_ref, mod_ref, g_ref, h_ref):
    h_ref[...] = _norm_mod(x_ref[...], g_ref[...], mod_ref[0:1, :], mod_ref[1:2, :]).astype(BF16)


def _prenorm(x2d, mod, norm_g, n_tok):
    t = x2d.shape[0]
    tm = min(512, n_tok)
    tiles_per_b = n_tok // tm
    per_batch = mod.shape[0] > 1
    return pl.pallas_call(
        _prenorm_kernel,
        grid=(t // tm,),
        in_specs=[pl.BlockSpec((tm, D_MODEL), lambda i: (i, 0)),
                  pl.BlockSpec((None, 6, D_MODEL),
                               (lambda i: (i // tiles_per_b, 0, 0)) if per_batch else (lambda i: (0, 0, 0))),
                  pl.BlockSpec((1, D_MODEL), lambda i: (0, 0))],
        out_specs=pl.BlockSpec((tm, D_MODEL), lambda i: (i, 0)),
        out_shape=jax.ShapeDtypeStruct((t, D_MODEL), BF16),
        compiler_params=_cparams("arbitrary"),
        name="prenorm",
    )(x2d, mod, norm_g)


PROJ_TN = 1024


def _rope_tables(n_lat):
    pos = jnp.arange(n_lat, dtype=jnp.int32)
    row = (pos // GRID_W).astype(F32)
    col = (pos % GRID_W).astype(F32)
    inv = 1.0 / (ROPE_BASE ** (jnp.arange(0, ROPE_AXIS_DIM, 2, dtype=F32) / ROPE_AXIS_DIM))
    lane = jnp.arange(LANES)
    d = lane % HEAD_DIM
    freq = inv[d % (ROPE_AXIS_DIM // 2)]
    ang = jnp.where(d < ROPE_AXIS_DIM, row[:, None], col[:, None]) * freq[None, :]
    first = (lane % ROPE_AXIS_DIM) < (ROPE_AXIS_DIM // 2)
    cos, sin = jnp.cos(ang), jnp.sin(ang)
    return cos, jnp.where(first, -sin, 0.0), jnp.where(first, 0.0, sin)


PROJ_SUB = 256
N_LAT_SLABS = (2 * D_QK + D_ATTN + 2 * D_MODEL) // PROJ_TN


def _proj_lat_kernel(x_ref, mod_ref, g_ref, w_ref, cos_ref, sa_ref, sb_ref,
                     h_ref, qk_ref, v_ref, gate_ref, h_scr):
    j = pl.program_id(1)

    def chains(kind):
        for r in range(x_ref.shape[0] // PROJ_SUB):
            rows = slice(r * PROJ_SUB, (r + 1) * PROJ_SUB)
            if kind == "q":
                hr = _norm_mod(x_ref[rows, :], g_ref[...], mod_ref[0:1, :], mod_ref[1:2, :]).astype(BF16)
                h_scr[rows, :] = hr
                h_ref[rows, :] = hr
            acc = _dot(h_scr[rows, :], w_ref[...])
            if kind in ("q", "k"):
                scale = Q_SCALE if kind == "q" else 1.0
                cos, sa, sb = cos_ref[rows, :] * scale, sa_ref[rows, :] * scale, sb_ref[rows, :] * scale
                for hh in range(N_HEADS):
                    a = acc[:, hh * LANES:(hh + 1) * LANES]
                    rot = a * cos + pltpu.roll(a, LANES - 16, 1) * sa + pltpu.roll(a, 16, 1) * sb
                    qk_ref[hh, rows, :] = rot.astype(BF16)
            elif kind == "v":
                for hh in range(N_HEADS):
                    v_ref[hh, rows, :] = acc[:, hh * LANES:(hh + 1) * LANES].astype(BF16)
            else:
                gate_ref[rows, :] = _sigmoid(acc).astype(BF16)

    pl.when(j == 0)(lambda: chains("q"))
    pl.when(j == 1)(lambda: chains("k"))
    pl.when(j == 2)(lambda: chains("v"))
    pl.when(j >= 3)(lambda: chains("gate"))


def _proj_lat(x2d, mod, norm_g, w_in, n_lat):
    t = x2d.shape[0]
    tm = min(1024, n_lat)
    tiles_per_b = n_lat // tm
    tab = pl.BlockSpec((tm, LANES), lambda i, j: (i % tiles_per_b, 0))
    col0 = D_SSM // PROJ_TN
    heads = lambda imap: pl.BlockSpec((None, N_HEADS, tm, LANES), imap)
    return pl.pallas_call(
        _proj_lat_kernel,
        grid=(t // tm, N_LAT_SLABS),
        in_specs=[pl.BlockSpec((tm, D_MODEL), lambda i, j: (i, 0)),
                  pl.BlockSpec((None, 6, D_MODEL), lambda i, j: (i // tiles_per_b, 0, 0)),
                  pl.BlockSpec((1, D_MODEL), lambda i, j: (0, 0)),
                  pl.BlockSpec((D_MODEL, PROJ_TN), lambda i, j: (0, col0 + j)),
                  tab, tab, tab],
        out_specs=[pl.BlockSpec((tm, D_MODEL), lambda i, j: (i, 0)),
                   heads(lambda i, j: (jnp.minimum(j, 1), 0, i, 0)),
                   heads(lambda i, j: (0, 0, i, 0)),
                   pl.BlockSpec((tm, PROJ_TN), lambda i, j: (i, jnp.maximum(j - 3, 0)))],
        out_shape=[jax.ShapeDtypeStruct((t, D_MODEL), BF16),
                   jax.ShapeDtypeStruct((2, N_HEADS, t, LANES), BF16),
                   jax.ShapeDtypeStruct((1, N_HEADS, t, LANES), BF16),
                   jax.ShapeDtypeStruct((t, 2 * D_MODEL), BF16)],
        scratch_shapes=[pltpu.VMEM((tm, D_MODEL), BF16)],
        compiler_params=_cparams("arbitrary", "arbitrary"),
        name="proj_lat",
    )(x2d, mod, norm_g, w_in, *_rope_tables(n_lat))


def _proj_heads_kernel(h_ref, w_ref, o_ref):
    for c in range(PROJ_TN // MXU_DIM):
        acc = _dot(h_ref[...], w_ref[:, c * MXU_DIM:(c + 1) * MXU_DIM])
        for hh in range(MXU_DIM // LANES):
            o_ref[c * (MXU_DIM // LANES) + hh] = acc[:, hh * LANES:(hh + 1) * LANES].astype(BF16)


def _proj_heads(h, w_in, col0, n_out):
    t = h.shape[0]
    tm = min(1024, t)
    return pl.pallas_call(
        _proj_heads_kernel,
        grid=(t // tm, n_out),
        in_specs=[pl.BlockSpec((tm, D_MODEL), lambda i, j: (i, 0)),
                  pl.BlockSpec((D_MODEL, PROJ_TN), lambda i, j: (0, col0 + j))],
        out_specs=pl.BlockSpec((None, N_HEADS, tm, LANES), lambda i, j: (j, 0, i, 0)),
        out_shape=jax.ShapeDtypeStruct((n_out, N_HEADS, t, LANES), BF16),
        compiler_params=_cparams("arbitrary", "arbitrary"),
        name="proj_heads",
    )(h, w_in)


U_ROWS = 512
GROUPS_PER_TILE = LANES // GROUP_CH


def _proj_u_kernel(h_ref, w_ref, z_ref, u_ref):
    n_tok = h_ref.shape[0]
    nch = n_tok // SSM_CHUNK
    for r in range(n_tok // U_ROWS):
        rows = slice(r * U_ROWS, (r + 1) * U_ROWS)
        u = _dot(h_ref[rows, :], w_ref[...])
        for lt in range(D_SSM // LANES):
            u_ref[lt, rows, :] = u[:, lt * LANES:(lt + 1) * LANES]
    for t in range(SSM_CHUNK):
        for lt in range(D_SSM // LANES):
            comb = u_ref[lt, pl.ds(t, nch, stride=SSM_CHUNK), :]
            z_ref[lt * GROUPS_PER_TILE:(lt + 1) * GROUPS_PER_TILE, t] = (
                comb.T.reshape(GROUPS_PER_TILE, GROUP_CH, nch).astype(BF16))


def _proj_u(h, w_u):
    t = h.shape[0]
    tm = min(SSM_CHUNK * LANES, t)
    nch = tm // SSM_CHUNK
    return pl.pallas_call(
        _proj_u_kernel,
        grid=(t // tm,),
        in_specs=[pl.BlockSpec((tm, D_MODEL), lambda i: (i, 0)),
                  pl.BlockSpec((D_MODEL, D_SSM), lambda i: (0, 0), pipeline_mode=pl.Buffered(1))],
        out_specs=pl.BlockSpec((SSM_GROUPS, SSM_CHUNK, GROUP_CH, nch), lambda i: (0, 0, 0, i)),
        out_shape=jax.ShapeDtypeStruct((SSM_GROUPS, SSM_CHUNK, GROUP_CH, t // SSM_CHUNK), BF16),
        scratch_shapes=[pltpu.VMEM((D_SSM // LANES, tm, LANES), F32)],
        compiler_params=_cparams("arbitrary"),
        name="proj_u",
    )(h, w_u)


def _ssm_param_kernel(a_re_ref, a_im_ref, ldt_ref, bt_ref, btsw_ref, cr_ref, ci_ref, d_ref,
                      m_ref, w_ref, v_ref, a_ref):
    tc = SSM_CHUNK
    lane = lax.broadcasted_iota(jnp.int32, (1, STATE_W), 1)
    lo = lane < SSM_STATE
    kts = []
    for d in range(2):
        dt = jnp.exp(ldt_ref[d:d + 1, :])
        lre = a_re_ref[d:d + 1, :] * dt
        lim = a_im_ref[d:d + 1, :] * dt
        kk = lax.broadcasted_iota(jnp.int32, (tc + 1, STATE_W), 0).astype(F32)
        mag = jnp.exp(kk * lre)
        c2 = mag * jnp.cos(kk * lim)
        s2 = mag * jnp.sin(kk * lim)
        xr = c2[1:2] - 1.0
        xi = s2[1:2]
        ar = a_re_ref[d:d + 1, :]
        ai = a_im_ref[d:d + 1, :]
        den = ar * ar + ai * ai
        er = (xr * ar + xi * ai) / den
        ei = (xi * ar - xr * ai) / den
        e2 = jnp.where(lo, -ei, ei)
        bt = bt_ref[d]
        btsw = btsw_ref[d]
        bb = er * bt + e2 * btsw
        bbsw = er * btsw - e2 * bt
        cdr = cr_ref[d]
        cdi = ci_ref[d]
        cneg = jnp.where(lo, cdr, -cdi)
        p2 = jnp.where(lo, -s2, s2)
        q1 = jnp.where(lo, c2, -s2)
        q2 = jnp.where(lo, -s2, -c2)
        gl = []
        for s in range(tc):
            kw = tc - 1 - s if d == 0 else s
            wd = c2[kw:kw + 1] * bb + p2[kw:kw + 1] * bbsw
            wsw = c2[kw:kw + 1] * bbsw - p2[kw:kw + 1] * bb
            w_ref[s, :, (2 * d) * STATE_W:(2 * d + 1) * STATE_W] = wd.astype(w_ref.dtype)
            w_ref[s, :, (2 * d + 1) * STATE_W:(2 * d + 2) * STATE_W] = wsw.astype(w_ref.dtype)
            kv = s + 1 if d == 0 else tc - s
            v_ref[s, :, d * STATE_W:(d + 1) * STATE_W] = (
                cdr * q1[kv:kv + 1] + cdi * q2[kv:kv + 1]).astype(v_ref.dtype)
            kl = tc - 1 - s if d == 0 else s
            gl.append(c2[kl:kl + 1] * bb + p2[kl:kl + 1] * bbsw)
        gcat = jnp.concatenate(gl, axis=0)
        kts.append(_dot_nt(cneg, gcat, precision=lax.Precision.HIGHEST))
        a_ref[d, 0:1, :] = c2[tc:tc + 1]
        a_ref[d, 1:2, :] = p2[tc:tc + 1]
    lane_w = lax.broadcasted_iota(jnp.int32, (GROUP_CH, CHUNK_W), 1)
    row_w = lax.broadcasted_iota(jnp.int32, (GROUP_CH, CHUNK_W), 0)
    dcol = d_ref[...]
    for t in range(tc):
        sh = (CHUNK_W - GROUP_CH * (tc - 1 - t)) % CHUNK_W
        fwd = kts[0] if sh == 0 else pltpu.roll(kts[0], sh, 1)
        fwd = jnp.where(lane_w < GROUP_CH * (t + 1), fwd, 0.0)
        rev = kts[1] if t == 0 else pltpu.roll(kts[1], GROUP_CH * t, 1)
        rev = jnp.where(lane_w >= GROUP_CH * t, rev, 0.0)
        skip = jnp.where(lane_w == GROUP_CH * t + row_w, dcol, 0.0)
        m_ref[t] = (fwd + rev + skip).astype(m_ref.dtype)


def _ssm_params(a_re, a_im, log_dt, b_re, b_im, c_re, c_im, d_skip):
    g = SSM_GROUPS
    dup = lambda a: jnp.concatenate([a, a], axis=-1)
    a_re2 = dup(jnp.transpose(a_re, (1, 0, 2)))
    a_im2 = dup(jnp.transpose(a_im, (1, 0, 2)))
    ldt = jnp.transpose(log_dt, (1, 0))[:, :, None]
    btr = jnp.transpose(b_re, (1, 0, 3, 2))
    bti = jnp.transpose(b_im, (1, 0, 3, 2))
    bt = jnp.concatenate([btr, bti], axis=-1)
    btsw = jnp.concatenate([bti, btr], axis=-1)
    cdr = dup(jnp.transpose(c_re, (1, 0, 2, 3)))
    cdi = dup(jnp.transpose(c_im, (1, 0, 2, 3)))
    dcol = d_skip.reshape(g, GROUP_CH, 1)
    g3 = lambda n: pl.BlockSpec((None, 2, n), lambda i: (i, 0, 0))
    g4 = lambda r, n: pl.BlockSpec((None, 2, r, n), lambda i: (i, 0, 0, 0))
    tc = SSM_CHUNK
    out = lambda n: pl.BlockSpec((None, tc, GROUP_CH, n), lambda i: (i, 0, 0, 0))
    m, w, v, a = pl.pallas_call(
        _ssm_param_kernel,
        grid=(g,),
        in_specs=[g3(STATE_W), g3(STATE_W), g3(1),
                  g4(GROUP_CH, STATE_W), g4(GROUP_CH, STATE_W),
                  g4(GROUP_CH, STATE_W), g4(GROUP_CH, STATE_W),
                  pl.BlockSpec((None, GROUP_CH, 1), lambda i: (i, 0, 0))],
        out_specs=[out(CHUNK_W), out(4 * STATE_W), out(2 * STATE_W),
                   pl.BlockSpec((None, 2, 2, STATE_W), lambda i: (i, 0, 0, 0))],
        out_shape=[jax.ShapeDtypeStruct((g, tc, GROUP_CH, CHUNK_W), BF16),
                   jax.ShapeDtypeStruct((g, tc, GROUP_CH, 4 * STATE_W), BF16),
                   jax.ShapeDtypeStruct((g, tc, GROUP_CH, 2 * STATE_W), BF16),
                   jax.ShapeDtypeStruct((g, 2, 2, STATE_W), F32)],
        compiler_params=_cparams("arbitrary"),
        name="ssm_params",
    )(a_re2, a_im2, ldt, bt, btsw, cdr, cdi, dcol)
    return (m.reshape(g, CHUNK_W, CHUNK_W), w.reshape(g, CHUNK_W, 4 * STATE_W),
            v.reshape(g, CHUNK_W, 2 * STATE_W), a)


ROW_PAD = SUBLANES


def _ssm_kernel(bsz, zl_ref, zc_ref, m_ref, w_ref, v_ref, a_ref, y_ref, sl_ref, sc_ref, h_ref):
    ncl = zl_ref.shape[-1] // bsz
    ncc = zc_ref.shape[-1] // bsz
    pl_l = ncl + ROW_PAD
    pl_c = ncc + ROW_PAD
    zl = zl_ref[...].reshape(CHUNK_W, bsz * ncl)
    zc = zc_ref[...].reshape(CHUNK_W, bsz * ncc)
    w = w_ref[...]
    s_c = _dot_tn(zc, w)
    for b in range(bsz):
        s_l = _dot_tn(zl[:, b * ncl:(b + 1) * ncl], w)
        for k in range(4):
            sl_ref[k, b * pl_l:b * pl_l + ncl, :] = s_l[:, k * STATE_W:(k + 1) * STATE_W]
            sc_ref[k, b * pl_c:b * pl_c + ncc, :] = s_c[b * ncc:(b + 1) * ncc,
                                                        k * STATE_W:(k + 1) * STATE_W]

    a1 = (a_ref[0, 0:1, :], a_ref[1, 0:1, :])
    a2 = (a_ref[0, 1:2, :], a_ref[1, 1:2, :])

    def step(d, h, hsw, s, ssw):
        return a1[d] * h + a2[d] * hsw + s, a1[d] * hsw - a2[d] * h + ssw

    def ctx_body(j, hs):
        hf, hfs, hr, hrs = hs
        rf = pl.ds(j, bsz, stride=pl_c)
        rr = pl.ds(ncc - 1 - j, bsz, stride=pl_c)
        hf, hfs = step(0, hf, hfs, sc_ref[0, rf, :], sc_ref[1, rf, :])
        hr, hrs = step(1, hr, hrs, sc_ref[2, rr, :], sc_ref[3, rr, :])
        return hf, hfs, hr, hrs

    zero = jnp.zeros((bsz, STATE_W), F32)
    hs = lax.fori_loop(0, ncc, ctx_body, (zero, zero, zero, zero))

    def lat_body(j, hs):
        hf, hfs, hr, hrs = hs
        rf = pl.ds(j, bsz, stride=pl_l)
        rr = pl.ds(ncl - 1 - j, bsz, stride=pl_l)
        h_ref[0, rf, :] = hf
        h_ref[1, rr, :] = hr
        hf, hfs = step(0, hf, hfs, sl_ref[0, rf, :], sl_ref[1, rf, :])
        hr, hrs = step(1, hr, hrs, sl_ref[2, rr, :], sl_ref[3, rr, :])
        return hf, hfs, hr, hrs

    lax.fori_loop(0, ncl, lat_body, hs)
    hcat = jnp.concatenate(
        [jnp.concatenate([h_ref[0, b * pl_l:b * pl_l + ncl, :], h_ref[1, b * pl_l:b * pl_l + ncl, :]],
                         axis=1) for b in range(bsz)], axis=0).astype(BF16)
    y = _dot(m_ref[...], zl) + _dot_nt(v_ref[...], hcat)
    y_ref[...] = y.reshape(SSM_CHUNK, GROUP_CH, bsz * ncl).astype(y_ref.dtype)


def _ssm_scan(zl, zc, m, w, v, a, bsz):
    g = zl.shape[0]
    nl = zl.shape[-1]
    nc = zc.shape[-1]
    blk = lambda *s: pl.BlockSpec((None,) + s, lambda i: (i,) + (0,) * len(s))
    return pl.pallas_call(
        functools.partial(_ssm_kernel, bsz),
        grid=(g,),
        in_specs=[blk(SSM_CHUNK, GROUP_CH, nl), blk(SSM_CHUNK, GROUP_CH, nc), blk(CHUNK_W, CHUNK_W),
                  blk(CHUNK_W, 4 * STATE_W), blk(CHUNK_W, 2 * STATE_W), blk(2, 2, STATE_W)],
        out_specs=blk(SSM_CHUNK, GROUP_CH, nl),
        out_shape=jax.ShapeDtypeStruct((g, SSM_CHUNK, GROUP_CH, nl), BF16),
        scratch_shapes=[pltpu.VMEM((4, nl + bsz * ROW_PAD, STATE_W), F32),
                        pltpu.VMEM((4, nc + bsz * ROW_PAD, STATE_W), F32),
                        pltpu.VMEM((2, nl + bsz * ROW_PAD, STATE_W), F32)],
        compiler_params=_cparams("arbitrary"),
        name="ssm_scan",
    )(zl, zc, m, w, v, a)


ATTN_SUB = 64


def _attn_kernel(lam_ref, g_ref, q_ref, kl_ref, vl_ref, kc_ref, vc_ref, o_ref, k_scr, v_scr):
    tq = q_ref.shape[0]
    n_lat = kl_ref.shape[0]
    n_ctx = kc_ref.shape[0]

    @pl.when(pl.program_id(2) == 0)
    def _():
        k_scr[0:n_lat] = kl_ref[...]
        k_scr[n_lat:n_lat + n_ctx] = kc_ref[...]
        v_scr[0:n_lat, 0:V_DIM] = vl_ref[...]
        v_scr[n_lat:n_lat + n_ctx, 0:V_DIM] = vc_ref[...]
        v_scr[:, V_DIM:2 * V_DIM] = jnp.ones((n_lat + n_ctx, V_DIM), BF16)

    lp = lam_ref[...]
    lam = (jnp.exp(jnp.sum(lp[0:1] * lp[1:2], axis=1, keepdims=True))
           - jnp.exp(jnp.sum(lp[2:3] * lp[3:4], axis=1, keepdims=True)) + LAM_INIT)
    gain = g_ref[...] * (1.0 - LAM_INIT)
    lane = lax.broadcasted_iota(jnp.int32, (ATTN_SUB, LANES), 1)
    zero = jnp.zeros((ATTN_SUB, LANES), BF16)
    for c in range(tq // ATTN_SUB):
        q = q_ref[c * ATTN_SUB:(c + 1) * ATTN_SUB, :]
        qq = jnp.concatenate([jnp.where(lane < HEAD_DIM, q, zero),
                              jnp.where(lane >= HEAD_DIM, q, zero)], axis=0)
        s = _dot_nt(qq, k_scr[...])
        mx = jnp.max(s, axis=-1, keepdims=True)
        p = jnp.exp2((s - mx).astype(BF16))
        oe = _dot(p, v_scr[...])
        o2 = oe[:, 0:V_DIM] / oe[:, V_DIM:2 * V_DIM]
        o = o2[0:ATTN_SUB] - lam * o2[ATTN_SUB:2 * ATTN_SUB]
        o = o * lax.rsqrt(jnp.mean(o * o, axis=-1, keepdims=True) + EPS) * gain
        o_ref[c * ATTN_SUB:(c + 1) * ATTN_SUB, :] = o.astype(o_ref.dtype)


def _attention(lam_p, subln_g, qk, v, kvc, bsz, n_lat, n_ctx):
    tq = min(2048, n_lat)
    nq = n_lat // tq
    t = bsz * n_lat
    sel = lambda which, rows: pl.BlockSpec((None, None, rows, LANES),
                                           lambda b, h, i: (which, h, b, 0))
    return pl.pallas_call(
        _attn_kernel,
        grid=(bsz, N_HEADS, nq),
        scratch_shapes=[pltpu.VMEM((n_lat + n_ctx, LANES), BF16),
                        pltpu.VMEM((n_lat + n_ctx, 2 * V_DIM), BF16)],
        in_specs=[pl.BlockSpec((4, HEAD_DIM), lambda b, h, i: (0, 0)),
                  pl.BlockSpec((1, V_DIM), lambda b, h, i: (0, 0)),
                  pl.BlockSpec((None, None, tq, LANES), lambda b, h, i: (0, h, b * nq + i, 0)),
                  sel(1, n_lat), sel(0, n_lat), sel(0, n_ctx), sel(1, n_ctx)],
        out_specs=pl.BlockSpec((None, tq, LANES), lambda b, h, i: (h, b * nq + i, 0)),
        out_shape=jax.ShapeDtypeStruct((N_HEADS, t, LANES), BF16),
        compiler_params=_cparams("arbitrary", "arbitrary", "arbitrary"),
        name="diff_attn",
    )(lam_p, subln_g, qk, qk, v, kvc, kvc)


def _unchunk_kernel(y_ref, o_ref, nat_ref):
    nch = y_ref.shape[-1]
    for t in range(SSM_CHUNK):
        yt = y_ref[:, t].reshape(D_SSM, nch).astype(F32).T
        for lt in range(D_SSM // LANES):
            nat_ref[lt, pl.ds(t, nch, stride=SSM_CHUNK), :] = yt[:, lt * LANES:(lt + 1) * LANES]
    for lt in range(D_SSM // LANES):
        o_ref[:, lt * LANES:(lt + 1) * LANES] = nat_ref[lt].astype(o_ref.dtype)


def _unchunk(yt, bsz, n_lat):
    nch = n_lat // SSM_CHUNK
    return pl.pallas_call(
        _unchunk_kernel,
        grid=(bsz,),
        in_specs=[pl.BlockSpec((SSM_GROUPS, SSM_CHUNK, GROUP_CH, nch), lambda b: (0, 0, 0, b))],
        out_specs=pl.BlockSpec((n_lat, D_SSM), lambda b: (b, 0)),
        out_shape=jax.ShapeDtypeStruct((bsz * n_lat, D_SSM), BF16),
        scratch_shapes=[pltpu.VMEM((D_SSM // LANES, n_lat, LANES), F32)],
        compiler_params=_cparams("arbitrary"),
        name="unchunk",
    )(yt)


MERGE_SUB = 256


def _merge_kernel(y_ref, o_ref, g_ref, x_ref, mod_ref, wg_ref, bg_ref, wps_ref, wpa_ref, wo_ref,
                  out_ref):
    for r in range(x_ref.shape[0] // MERGE_SUB):
        rows = slice(r * MERGE_SUB, (r + 1) * MERGE_SUB)
        y = y_ref[rows, :].astype(F32)
        z = 0.5 * y * (1.0 + jnp.tanh(math.sqrt(2.0 / math.pi) * (y + 0.044715 * (y * y * y))))
        gl = _dot(z.astype(BF16), wg_ref[...]) + bg_ref[...]
        ys = (z * _sigmoid(gl)).astype(BF16)
        oa = jnp.concatenate([o_ref[h, rows, :] for h in range(N_HEADS)], axis=1)
        merged = (g_ref[rows, 0:D_MODEL].astype(F32) * _dot(ys, wps_ref[...])
                  + g_ref[rows, D_MODEL:2 * D_MODEL].astype(F32) * _dot(oa, wpa_ref[...]))
        out = _dot(merged.astype(BF16), wo_ref[...])
        out_ref[rows, :] = x_ref[rows, :] + mod_ref[2:3, :] * out


def _merge(y, o, gates, x2d, mod, w_glu, b_glu, w_ps, w_pa, w_o, n_lat):
    t = x2d.shape[0]
    tm = min(512, n_lat)
    tiles_per_b = n_lat // tm
    row = lambda i: (i, 0)
    const = lambda shape: pl.BlockSpec(shape, lambda i: (0,) * len(shape),
                                       pipeline_mode=pl.Buffered(1))
    return pl.pallas_call(
        _merge_kernel,
        grid=(t // tm,),
        in_specs=[pl.BlockSpec((tm, D_SSM), row),
                  pl.BlockSpec((N_HEADS, tm, LANES), lambda i: (0, i, 0)),
                  pl.BlockSpec((tm, 2 * D_MODEL), row),
                  pl.BlockSpec((tm, D_MODEL), row),
                  pl.BlockSpec((None, 6, D_MODEL), lambda i: (i // tiles_per_b, 0, 0)),
                  const((D_SSM, D_SSM)), const((1, D_SSM)),
                  const((D_SSM, D_MODEL)), const((D_ATTN, D_MODEL)), const((D_MODEL, D_MODEL))],
        out_specs=pl.BlockSpec((tm, D_MODEL), row),
        out_shape=jax.ShapeDtypeStruct((t, D_MODEL), F32),
        compiler_params=_cparams("arbitrary"),
        name="merge",
    )(y, o, gates, x2d, mod, w_glu, b_glu, w_ps, w_pa, w_o)


FFN_SUB = 256


def _ffn_kernel(x_ref, mod_ref, g2_ref, gf_ref, wg_ref, wu_ref, wo_ref, out_ref, h_ref):
    j = pl.program_id(1)
    last = pl.num_programs(1) - 1

    def chains(first, final):
        for r in range(x_ref.shape[0] // FFN_SUB):
            rows = slice(r * FFN_SUB, (r + 1) * FFN_SUB)
            if first:
                h_ref[rows, :] = _norm_mod(x_ref[rows, :], g2_ref[...], mod_ref[3:4, :],
                                           mod_ref[4:5, :]).astype(BF16)
            h = h_ref[rows, :]
            gate = _dot(h, wg_ref[...])
            up = _dot(h, wu_ref[...])
            act = (gate * _sigmoid(gate) * up).astype(BF16)
            part = _dot(act, wo_ref[...])
            if first:
                out_ref[rows, :] = part
            elif not final:
                out_ref[rows, :] += part
            else:
                xo = x_ref[rows, :] + mod_ref[5:6, :] * (out_ref[rows, :] + part)
                ms = jnp.mean(xo * xo, axis=-1, keepdims=True)
                out_ref[rows, :] = xo * lax.rsqrt(ms + EPS) * gf_ref[...]

    pl.when(j == 0)(lambda: chains(True, False))
    pl.when((j > 0) & (j < last))(lambda: chains(False, False))
    pl.when(j == last)(lambda: chains(False, True))


FFN_TF = 512


def _ffn(x_mid, mod, norm2_g, norm_f_g, w_ffn_in, w_ffn_out, n_lat):
    t = x_mid.shape[0]
    tm = min(512, n_lat)
    tf = FFN_TF
    nf = D_FF // tf
    assert nf >= 2, "the kernel needs distinct first and last hidden-dimension steps"
    tiles_per_b = n_lat // tm
    row = lambda i, j: (i, 0)
    return pl.pallas_call(
        _ffn_kernel,
        grid=(t // tm, nf),
        in_specs=[pl.BlockSpec((tm, D_MODEL), row),
                  pl.BlockSpec((None, 6, D_MODEL), lambda i, j: (i // tiles_per_b, 0, 0)),
                  pl.BlockSpec((1, D_MODEL), lambda i, j: (0, 0)),
                  pl.BlockSpec((1, D_MODEL), lambda i, j: (0, 0)),
                  pl.BlockSpec((D_MODEL, tf), lambda i, j: (0, j)),
                  pl.BlockSpec((D_MODEL, tf), lambda i, j: (0, j + nf)),
                  pl.BlockSpec((tf, D_MODEL), lambda i, j: (j, 0))],
        out_specs=pl.BlockSpec((tm, D_MODEL), row),
        out_shape=jax.ShapeDtypeStruct((t, D_MODEL), F32),
        scratch_shapes=[pltpu.VMEM((tm, D_MODEL), BF16)],
        compiler_params=_cparams("arbitrary", "arbitrary"),
        name="ffn",
    )(x_mid, mod, norm2_g, norm_f_g, w_ffn_in, w_ffn_in, w_ffn_out)


def kernel(x, c, ctx, c_ctx, w_ada, b_ada, norm1_g, w_in, ssm_a_re, ssm_a_im, ssm_log_dt, ssm_b_re, ssm_b_im, ssm_c_re, ssm_c_im, ssm_d, w_glu, b_glu, lambda_q1, lambda_k1, lambda_q2, lambda_k2, subln_g, w_proj_ssm, w_proj_attn, w_out, norm2_g, w_ffn_in, w_ffn_out, norm_f_g):
    bsz, n_lat, d = x.shape
    n_ctx = ctx.shape[1]
    assert d == D_MODEL and w_ada.shape[0] == 1, "single-layer block only"
    assert n_lat % GRID_W == 0 and n_lat % SSM_CHUNK == 0 and n_ctx % SSM_CHUNK == 0
    assert bsz % SUBLANES == 0

    pad = (-(bsz + 1)) % SUBLANES
    cc = jnp.concatenate([c, c_ctx[None, :], jnp.zeros((pad, d), F32)], axis=0)
    mod_all = _adaln(cc, w_ada[0], b_ada[0])
    mod = mod_all[:bsz].reshape(bsz, 6, d)
    mod_c = mod_all[bsz:bsz + 1].reshape(1, 6, d)

    w_in_b = w_in[0].astype(BF16)
    g1 = norm1_g[0].reshape(1, d)
    x2d = x.reshape(bsz * n_lat, d)
    h, qk, v, gates = _proj_lat(x2d, mod, g1, w_in_b, n_lat)
    hc = _prenorm(ctx.reshape(bsz * n_ctx, d), mod_c, g1, n_ctx)
    kvc = _proj_heads(hc, w_in_b, (D_SSM + D_QK) // PROJ_TN, 2)
    zl = _proj_u(h, w_in_b)
    zc = _proj_u(hc, w_in_b)

    m, w, vv, a = _ssm_params(ssm_a_re[0], ssm_a_im[0], ssm_log_dt[0], ssm_b_re[0], ssm_b_im[0],
                              ssm_c_re[0], ssm_c_im[0], ssm_d[0])
    yt = _ssm_scan(zl, zc, m, w, vv, a, bsz)

    lam_p = jnp.stack([lambda_q1[0], lambda_k1[0], lambda_q2[0], lambda_k2[0]]).astype(F32)
    o = _attention(lam_p, subln_g[0].reshape(1, V_DIM), qk, v, kvc, bsz, n_lat, n_ctx)

    y = _unchunk(yt, bsz, n_lat)
    x_mid = _merge(y, o, gates, x2d, mod, w_glu[0].astype(BF16), b_glu[0].reshape(1, D_SSM),
                   w_proj_ssm[0].astype(BF16), w_proj_attn[0].astype(BF16), w_out[0].astype(BF16),
                   n_lat)
    out = _ffn(x_mid, mod, norm2_g[0].reshape(1, d), norm_f_g.reshape(1, d),
               w_ffn_in[0].astype(BF16), w_ffn_out[0].astype(BF16), n_lat)
    return out.reshape(bsz, n_lat, d)
```

```python
import functools
import math

import jax
import jax.numpy as jnp
from jax import lax
from jax.experimental import pallas as pl
from jax.experimental.pallas import tpu as pltpu

F32 = jnp.float32
BF16 = jnp.bfloat16

D_MODEL = 2048
GRID_W = 64
D_SSM = 1024
GROUP_CH = 16
SSM_GROUPS = D_SSM // GROUP_CH
SSM_STATE = 64
N_HEADS = 8
HEAD_DIM = 64
V_DIM = 2 * HEAD_DIM
D_QK = N_HEADS * 2 * HEAD_DIM
D_ATTN = N_HEADS * V_DIM
ATTN_SCALE = HEAD_DIM ** -0.5
Q_SCALE = ATTN_SCALE * math.log2(math.e)
ROPE_AXIS_DIM = HEAD_DIM // 2
ROPE_BASE = 10000.0
D_FF = 5632
D_IN = D_SSM + 2 * D_QK + D_ATTN + 2 * D_MODEL
EPS = 1e-6
LAM_INIT = 0.8 - 0.6 * math.exp(-0.3 * 0)

LANES = 128
SUBLANES = 8
MXU_DIM = 256
SSM_CHUNK = 16
CHUNK_W = SSM_CHUNK * GROUP_CH
STATE_W = 2 * SSM_STATE
VMEM_LIMIT = 56 * 1024 * 1024
VMEM_LIMIT_FFN = 60 * 1024 * 1024


def _cparams(*sem, vmem_limit=VMEM_LIMIT):
    return pltpu.CompilerParams(dimension_semantics=sem, vmem_limit_bytes=vmem_limit)


def _dot(a, b):
    return jnp.dot(a, b, preferred_element_type=F32)


def _dot_nt(a, b, precision=None):
    return lax.dot_general(a, b, (((1,), (1,)), ((), ())), preferred_element_type=F32,
                           precision=precision)


def _dot_tn(a, b):
    return lax.dot_general(a, b, (((0,), (0,)), ((), ())), preferred_element_type=F32)


def _sigmoid(x):
    return 1.0 / (1.0 + jnp.exp(-x))


def _adaln_kernel(c_ref, w_ref, b_ref, o_ref):
    c = c_ref[...]
    s = (c * _sigmoid(c)).astype(BF16)
    o_ref[...] = _dot(s, w_ref[...].astype(BF16)) + b_ref[...]


def _adaln(cc, w_ada, b_ada):
    rows, d = cc.shape
    n = w_ada.shape[1]
    tn = 1024
    return pl.pallas_call(
        _adaln_kernel,
        grid=(n // tn,),
        in_specs=[pl.BlockSpec((rows, d), lambda j: (0, 0)),
                  pl.BlockSpec((d, tn), lambda j: (0, j)),
                  pl.BlockSpec((1, tn), lambda j: (0, j))],
        out_specs=pl.BlockSpec((rows, tn), lambda j: (0, j)),
        out_shape=jax.ShapeDtypeStruct((rows, n), F32),
        compiler_params=_cparams("arbitrary"),
        name="adaln",
    )(cc, w_ada, b_ada.reshape(1, n))


def _norm_mod(x, g, shift, scale):
    ms = jnp.mean(x * x, axis=-1, keepdims=True)
    y = x * lax.rsqrt(ms + EPS) * g
    return y * (1.0 + scale) + shift


def _prenorm_kernel(x_ref, mod_ref, g_ref, h_ref):
    h_ref[...] = _norm_mod(x_ref[...], g_ref[...], mod_ref[0:1, :], mod_ref[1:2, :]).astype(BF16)


def _prenorm(x2d, mod, norm_g, n_tok):
    t = x2d.shape[0]
    tm = min(512, n_tok)
    tiles_per_b = n_tok // tm
    per_batch = mod.shape[0] > 1
    return pl.pallas_call(
        _prenorm_kernel,
        grid=(t // tm,),
        in_specs=[pl.BlockSpec((tm, D_MODEL), lambda i: (i, 0)),
                  pl.BlockSpec((None, 6, D_MODEL),
                               (lambda i: (i // tiles_per_b, 0, 0)) if per_batch else (lambda i: (0, 0, 0))),
                  pl.BlockSpec((1, D_MODEL), lambda i: (0, 0))],
        out_specs=pl.BlockSpec((tm, D_MODEL), lambda i: (i, 0)),
        out_shape=jax.ShapeDtypeStruct((t, D_MODEL), BF16),
        compiler_params=_cparams("arbitrary"),
        name="prenorm",
    )(x2d, mod, norm_g)


PROJ_TN = 1024


def _rope_tables(n_lat):
    pos = jnp.arange(n_lat, dtype=jnp.int32)
    row = (pos // GRID_W).astype(F32)
    col = (pos % GRID_W).astype(F32)
    inv = 1.0 / (ROPE_BASE ** (jnp.arange(0, ROPE_AXIS_DIM, 2, dtype=F32) / ROPE_AXIS_DIM))
    lane = jnp.arange(LANES)
    d = lane % HEAD_DIM
    freq = inv[d % (ROPE_AXIS_DIM // 2)]
    ang = jnp.where(d < ROPE_AXIS_DIM, row[:, None], col[:, None]) * freq[None, :]
    first = (lane % ROPE_AXIS_DIM) < (ROPE_AXIS_DIM // 2)
    cos, sin = jnp.cos(ang), jnp.sin(ang)
    return cos, jnp.where(first, -sin, 0.0), jnp.where(first, 0.0, sin)


PROJ_SUB = 256
N_LAT_SLABS = (2 * D_QK + D_ATTN + 2 * D_MODEL) // PROJ_TN


def _proj_lat_kernel(x_ref, mod_ref, g_ref, w_ref, cos_ref, sa_ref, sb_ref,
                     h_ref, qk_ref, v_ref, gate_ref, h_scr):
    j = pl.program_id(1)

    def chains(kind):
        for r in range(x_ref.shape[0] // PROJ_SUB):
            rows = slice(r * PROJ_SUB, (r + 1) * PROJ_SUB)
            if kind == "q":
                hr = _norm_mod(x_ref[rows, :], g_ref[...], mod_ref[0:1, :], mod_ref[1:2, :]).astype(BF16)
                h_scr[rows, :] = hr
                h_ref[rows, :] = hr
            acc = _dot(h_scr[rows, :], w_ref[...])
            if kind in ("q", "k"):
                scale = Q_SCALE if kind == "q" else 1.0
                cos, sa, sb = cos_ref[rows, :] * scale, sa_ref[rows, :] * scale, sb_ref[rows, :] * scale
                for hh in range(N_HEADS):
                    a = acc[:, hh * LANES:(hh + 1) * LANES]
                    rot = a * cos + pltpu.roll(a, LANES - 16, 1) * sa + pltpu.roll(a, 16, 1) * sb
                    qk_ref[hh, rows, :] = rot.astype(BF16)
            elif kind == "v":
                for hh in range(N_HEADS):
                    v_ref[hh, rows, :] = acc[:, hh * LANES:(hh + 1) * LANES].astype(BF16)
            else:
                gate_ref[rows, :] = _sigmoid(acc).astype(BF16)

    pl.when(j == 0)(lambda: chains("q"))
    pl.when(j == 1)(lambda: chains("k"))
    pl.when(j == 2)(lambda: chains("v"))
    pl.when(j >= 3)(lambda: chains("gate"))


def _proj_lat(x2d, mod, norm_g, w_in, n_lat):
    t = x2d.shape[0]
    tm = min(1024, n_lat)
    tiles_per_b = n_lat // tm
    tab = pl.BlockSpec((tm, LANES), lambda i, j: (i % tiles_per_b, 0))
    col0 = D_SSM // PROJ_TN
    heads = lambda imap: pl.BlockSpec((None, N_HEADS, tm, LANES), imap)
    return pl.pallas_call(
        _proj_lat_kernel,
        grid=(t // tm, N_LAT_SLABS),
        in_specs=[pl.BlockSpec((tm, D_MODEL), lambda i, j: (i, 0)),
                  pl.BlockSpec((None, 6, D_MODEL), lambda i, j: (i // tiles_per_b, 0, 0)),
                  pl.BlockSpec((1, D_MODEL), lambda i, j: (0, 0)),
                  pl.BlockSpec((D_MODEL, PROJ_TN), lambda i, j: (0, col0 + j)),
                  tab, tab, tab],
        out_specs=[pl.BlockSpec((tm, D_MODEL), lambda i, j: (i, 0)),
                   heads(lambda i, j: (jnp.minimum(j, 1), 0, i, 0)),
                   heads(lambda i, j: (0, 0, i, 0)),
                   pl.BlockSpec((tm, PROJ_TN), lambda i, j: (i, jnp.maximum(j - 3, 0)))],
        out_shape=[jax.ShapeDtypeStruct((t, D_MODEL), BF16),
                   jax.ShapeDtypeStruct((2, N_HEADS, t, LANES), BF16),
                   jax.ShapeDtypeStruct((1, N_HEADS, t, LANES), BF16),
                   jax.ShapeDtypeStruct((t, 2 * D_MODEL), BF16)],
        scratch_shapes=[pltpu.VMEM((tm, D_MODEL), BF16)],
        compiler_params=_cparams("arbitrary", "arbitrary"),
        name="proj_lat",
    )(x2d, mod, norm_g, w_in, *_rope_tables(n_lat))


def _proj_heads_kernel(h_ref, w_ref, o_ref):
    for c in range(PROJ_TN // MXU_DIM):
        acc = _dot(h_ref[...], w_ref[:, c * MXU_DIM:(c + 1) * MXU_DIM])
        for hh in range(MXU_DIM // LANES):
            o_ref[c * (MXU_DIM // LANES) + hh] = acc[:, hh * LANES:(hh + 1) * LANES].astype(BF16)


def _proj_heads(h, w_in, col0, n_out):
    t = h.shape[0]
    tm = min(1024, t)
    return pl.pallas_call(
        _proj_heads_kernel,
        grid=(t // tm, n_out),
        in_specs=[pl.BlockSpec((tm, D_MODEL), lambda i, j: (i, 0)),
                  pl.BlockSpec((D_MODEL, PROJ_TN), lambda i, j: (0, col0 + j))],
        out_specs=pl.BlockSpec((None, N_HEADS, tm, LANES), lambda i, j: (j, 0, i, 0)),
        out_shape=jax.ShapeDtypeStruct((n_out, N_HEADS, t, LANES), BF16),
        compiler_params=_cparams("arbitrary", "arbitrary"),
        name="proj_heads",
    )(h, w_in)


U_ROWS = 512
GROUPS_PER_TILE = LANES // GROUP_CH


def _proj_u_kernel(h_ref, w_ref, z_ref, u_ref):
    n_tok = h_ref.shape[0]
    nch = n_tok // SSM_CHUNK
    for r in range(n_tok // U_ROWS):
        rows = slice(r * U_ROWS, (r + 1) * U_ROWS)
        u = _dot(h_ref[rows, :], w_ref[...])
        for lt in range(D_SSM // LANES):
            u_ref[lt, rows, :] = u[:, lt * LANES:(lt + 1) * LANES]
    for t in range(SSM_CHUNK):
        for lt in range(D_SSM // LANES):
            comb = u_ref[lt, pl.ds(t, nch, stride=SSM_CHUNK), :]
            z_ref[lt * GROUPS_PER_TILE:(lt + 1) * GROUPS_PER_TILE, t] = (
                comb.T.reshape(GROUPS_PER_TILE, GROUP_CH, nch).astype(BF16))


def _proj_u(h, w_u):
    t = h.shape[0]
    tm = min(SSM_CHUNK * LANES, t)
    nch = tm // SSM_CHUNK
    return pl.pallas_call(
        _proj_u_kernel,
        grid=(t // tm,),
        in_specs=[pl.BlockSpec((tm, D_MODEL), lambda i: (i, 0)),
                  pl.BlockSpec((D_MODEL, D_SSM), lambda i: (0, 0), pipeline_mode=pl.Buffered(1))],
        out_specs=pl.BlockSpec((SSM_GROUPS, SSM_CHUNK, GROUP_CH, nch), lambda i: (0, 0, 0, i)),
        out_shape=jax.ShapeDtypeStruct((SSM_GROUPS, SSM_CHUNK, GROUP_CH, t // SSM_CHUNK), BF16),
        scratch_shapes=[pltpu.VMEM((D_SSM // LANES, tm, LANES), F32)],
        compiler_params=_cparams("arbitrary"),
        name="proj_u",
    )(h, w_u)


def _ssm_param_kernel(a_re_ref, a_im_ref, ldt_ref, bt_ref, btsw_ref, cr_ref, ci_ref, d_ref,
                      m_ref, w_ref, v_ref, a_ref):
    tc = SSM_CHUNK
    lane = lax.broadcasted_iota(jnp.int32, (1, STATE_W), 1)
    lo = lane < SSM_STATE
    kts = []
    for d in range(2):
        dt = jnp.exp(ldt_ref[d:d + 1, :])
        lre = a_re_ref[d:d + 1, :] * dt
        lim = a_im_ref[d:d + 1, :] * dt
        kk = lax.broadcasted_iota(jnp.int32, (tc + 1, STATE_W), 0).astype(F32)
        mag = jnp.exp(kk * lre)
        c2 = mag * jnp.cos(kk * lim)
        s2 = mag * jnp.sin(kk * lim)
        xr = c2[1:2] - 1.0
        xi = s2[1:2]
        ar = a_re_ref[d:d + 1, :]
        ai = a_im_ref[d:d + 1, :]
        den = ar * ar + ai * ai
        er = (xr * ar + xi * ai) / den
        ei = (xi * ar - xr * ai) / den
        e2 = jnp.where(lo, -ei, ei)
        bt = bt_ref[d]
        btsw = btsw_ref[d]
        bb = er * bt + e2 * btsw
        bbsw = er * btsw - e2 * bt
        cdr = cr_ref[d]
        cdi = ci_ref[d]
        cneg = jnp.where(lo, cdr, -cdi)
        p2 = jnp.where(lo, -s2, s2)
        q1 = jnp.where(lo, c2, -s2)
        q2 = jnp.where(lo, -s2, -c2)
        gl = []
        for s in range(tc):
            kw = tc - 1 - s if d == 0 else s
            wd = c2[kw:kw + 1] * bb + p2[kw:kw + 1] * bbsw
            wsw = c2[kw:kw + 1] * bbsw - p2[kw:kw + 1] * bb
            w_ref[s, :, (2 * d) * STATE_W:(2 * d + 1) * STATE_W] = wd.astype(w_ref.dtype)
            w_ref[s, :, (2 * d + 1) * STATE_W:(2 * d + 2) * STATE_W] = wsw.astype(w_ref.dtype)
            kv = s + 1 if d == 0 else tc - s
            v_ref[s, :, d * STATE_W:(d + 1) * STATE_W] = (
                cdr * q1[kv:kv + 1] + cdi * q2[kv:kv + 1]).astype(v_ref.dtype)
            kl = tc - 1 - s if d == 0 else s
            gl.append(c2[kl:kl + 1] * bb + p2[kl:kl + 1] * bbsw)
        gcat = jnp.concatenate(gl, axis=0)
        kts.append(_dot_nt(cneg, gcat, precision=lax.Precision.HIGHEST))
        a_ref[d, 0:1, :] = c2[tc:tc + 1]
        a_ref[d, 1:2, :] = p2[tc:tc + 1]
    lane_w = lax.broadcasted_iota(jnp.int32, (GROUP_CH, CHUNK_W), 1)
    row_w = lax.broadcasted_iota(jnp.int32, (GROUP_CH, CHUNK_W), 0)
    dcol = d_ref[...]
    for t in range(tc):
        sh = (CHUNK_W - GROUP_CH * (tc - 1 - t)) % CHUNK_W
        fwd = kts[0] if sh == 0 else pltpu.roll(kts[0], sh, 1)
        fwd = jnp.where(lane_w < GROUP_CH * (t + 1), fwd, 0.0)
        rev = kts[1] if t == 0 else pltpu.roll(kts[1], GROUP_CH * t, 1)
        rev = jnp.where(lane_w >= GROUP_CH * t, rev, 0.0)
        skip = jnp.where(lane_w == GROUP_CH * t + row_w, dcol, 0.0)
        m_ref[t] = (fwd + rev + skip).astype(m_ref.dtype)


def _ssm_params(a_re, a_im, log_dt, b_re, b_im, c_re, c_im, d_skip):
    g = SSM_GROUPS
    dup = lambda a: jnp.concatenate([a, a], axis=-1)
    a_re2 = dup(jnp.transpose(a_re, (1, 0, 2)))
    a_im2 = dup(jnp.transpose(a_im, (1, 0, 2)))
    ldt = jnp.transpose(log_dt, (1, 0))[:, :, None]
    btr = jnp.transpose(b_re, (1, 0, 3, 2))
    bti = jnp.transpose(b_im, (1, 0, 3, 2))
    bt = jnp.concatenate([btr, bti], axis=-1)
    btsw = jnp.concatenate([bti, btr], axis=-1)
    cdr = dup(jnp.transpose(c_re, (1, 0, 2, 3)))
    cdi = dup(jnp.transpose(c_im, (1, 0, 2, 3)))
    dcol = d_skip.reshape(g, GROUP_CH, 1)
    g3 = lambda n: pl.BlockSpec((None, 2, n), lambda i: (i, 0, 0))
    g4 = lambda r, n: pl.BlockSpec((None, 2, r, n), lambda i: (i, 0, 0, 0))
    tc = SSM_CHUNK
    out = lambda n: pl.BlockSpec((None, tc, GROUP_CH, n), lambda i: (i, 0, 0, 0))
    m, w, v, a = pl.pallas_call(
        _ssm_param_kernel,
        grid=(g,),
        in_specs=[g3(STATE_W), g3(STATE_W), g3(1),
                  g4(GROUP_CH, STATE_W), g4(GROUP_CH, STATE_W),
                  g4(GROUP_CH, STATE_W), g4(GROUP_CH, STATE_W),
                  pl.BlockSpec((None, GROUP_CH, 1), lambda i: (i, 0, 0))],
        out_specs=[out(CHUNK_W), out(4 * STATE_W), out(2 * STATE_W),
                   pl.BlockSpec((None, 2, 2, STATE_W), lambda i: (i, 0, 0, 0))],
        out_shape=[jax.ShapeDtypeStruct((g, tc, GROUP_CH, CHUNK_W), BF16),
                   jax.ShapeDtypeStruct((g, tc, GROUP_CH, 4 * STATE_W), BF16),
                   jax.ShapeDtypeStruct((g, tc, GROUP_CH, 2 * STATE_W), BF16),
                   jax.ShapeDtypeStruct((g, 2, 2, STATE_W), F32)],
        compiler_params=_cparams("arbitrary"),
        name="ssm_params",
    )(a_re2, a_im2, ldt, bt, btsw, cdr, cdi, dcol)
    return (m.reshape(g, CHUNK_W, CHUNK_W), w.reshape(g, CHUNK_W, 4 * STATE_W),
            v.reshape(g, CHUNK_W, 2 * STATE_W), a)


ROW_PAD = SUBLANES


SCAN_GROUPS = 2


def _ssm_kernel(bsz, zl_ref, zc_ref, m_ref, w_ref, v_ref, a_ref, y_ref, sl_ref, sc_ref, h_ref):
    ng = SCAN_GROUPS
    ncl = zl_ref.shape[-1] // bsz
    ncc = zc_ref.shape[-1] // bsz
    pl_l = ncl + ROW_PAD
    pl_c = ncc + ROW_PAD
    zls = [zl_ref[g].reshape(CHUNK_W, bsz * ncl) for g in range(ng)]
    for g in range(ng):
        w = w_ref[g]
        s_c = _dot_tn(zc_ref[g].reshape(CHUNK_W, bsz * ncc), w)
        for b in range(bsz):
            s_l = _dot_tn(zls[g][:, b * ncl:(b + 1) * ncl], w)
            for k in range(4):
                sl_ref[4 * g + k, b * pl_l:b * pl_l + ncl, :] = s_l[:, k * STATE_W:(k + 1) * STATE_W]
                sc_ref[4 * g + k, b * pl_c:b * pl_c + ncc, :] = s_c[b * ncc:(b + 1) * ncc,
                                                                    k * STATE_W:(k + 1) * STATE_W]

    a1 = [(a_ref[g, 0, 0:1, :], a_ref[g, 1, 0:1, :]) for g in range(ng)]
    a2 = [(a_ref[g, 0, 1:2, :], a_ref[g, 1, 1:2, :]) for g in range(ng)]

    def step(g, d, h, hsw, s, ssw):
        return a1[g][d] * h + a2[g][d] * hsw + s, a1[g][d] * hsw - a2[g][d] * h + ssw

    def body(s_ref, n, pitch, store):
        def run(j, hs):
            rf = pl.ds(j, bsz, stride=pitch)
            rr = pl.ds(n - 1 - j, bsz, stride=pitch)
            out = []
            for g in range(ng):
                hf, hfs, hr, hrs = hs[4 * g:4 * g + 4]
                if store:
                    h_ref[2 * g, rf, :] = hf
                    h_ref[2 * g + 1, rr, :] = hr
                hf, hfs = step(g, 0, hf, hfs, s_ref[4 * g, rf, :], s_ref[4 * g + 1, rf, :])
                hr, hrs = step(g, 1, hr, hrs, s_ref[4 * g + 2, rr, :], s_ref[4 * g + 3, rr, :])
                out += [hf, hfs, hr, hrs]
            return tuple(out)
        return run

    zero = jnp.zeros((bsz, STATE_W), F32)
    hs = lax.fori_loop(0, ncc, body(sc_ref, ncc, pl_c, False), (zero,) * (4 * ng))
    lax.fori_loop(0, ncl, body(sl_ref, ncl, pl_l, True), hs)
    for g in range(ng):
        hcat = jnp.concatenate(
            [jnp.concatenate([h_ref[2 * g, b * pl_l:b * pl_l + ncl, :],
                              h_ref[2 * g + 1, b * pl_l:b * pl_l + ncl, :]], axis=1)
             for b in range(bsz)], axis=0).astype(BF16)
        y = _dot(m_ref[g], zls[g]) + _dot_nt(v_ref[g], hcat)
        y_ref[g] = y.reshape(SSM_CHUNK, GROUP_CH, bsz * ncl).astype(y_ref.dtype)


def _ssm_scan(zl, zc, m, w, v, a, bsz):
    g = zl.shape[0]
    nl = zl.shape[-1]
    nc = zc.shape[-1]
    ng = SCAN_GROUPS
    blk = lambda *s: pl.BlockSpec((ng,) + s, lambda i: (i,) + (0,) * len(s))
    return pl.pallas_call(
        functools.partial(_ssm_kernel, bsz),
        grid=(g // ng,),
        in_specs=[blk(SSM_CHUNK, GROUP_CH, nl), blk(SSM_CHUNK, GROUP_CH, nc), blk(CHUNK_W, CHUNK_W),
                  blk(CHUNK_W, 4 * STATE_W), blk(CHUNK_W, 2 * STATE_W), blk(2, 2, STATE_W)],
        out_specs=blk(SSM_CHUNK, GROUP_CH, nl),
        out_shape=jax.ShapeDtypeStruct((g, SSM_CHUNK, GROUP_CH, nl), BF16),
        scratch_shapes=[pltpu.VMEM((4 * ng, nl + bsz * ROW_PAD, STATE_W), F32),
                        pltpu.VMEM((4 * ng, nc + bsz * ROW_PAD, STATE_W), F32),
                        pltpu.VMEM((2 * ng, nl + bsz * ROW_PAD, STATE_W), F32)],
        compiler_params=_cparams("arbitrary"),
        name="ssm_scan",
    )(zl, zc, m, w, v, a)


ATTN_SUB = 64


def _attn_kernel(lam_ref, g_ref, q_ref, kl_ref, vl_ref, kc_ref, vc_ref, o_ref, k_scr, v_scr):
    tq = q_ref.shape[0]
    n_lat = kl_ref.shape[0]
    n_ctx = kc_ref.shape[0]

    @pl.when(pl.program_id(2) == 0)
    def _():
        k_scr[0:n_lat] = kl_ref[...]
        k_scr[n_lat:n_lat + n_ctx] = kc_ref[...]
        v_scr[0:n_lat, 0:V_DIM] = vl_ref[...]
        v_scr[n_lat:n_lat + n_ctx, 0:V_DIM] = vc_ref[...]
        v_scr[:, V_DIM:2 * V_DIM] = jnp.ones((n_lat + n_ctx, V_DIM), BF16)

    lp = lam_ref[...]
    lam = (jnp.exp(jnp.sum(lp[0:1] * lp[1:2], axis=1, keepdims=True))
           - jnp.exp(jnp.sum(lp[2:3] * lp[3:4], axis=1, keepdims=True)) + LAM_INIT)
    gain = g_ref[...] * (1.0 - LAM_INIT)
    lane = lax.broadcasted_iota(jnp.int32, (ATTN_SUB, LANES), 1)
    zero = jnp.zeros((ATTN_SUB, LANES), BF16)
    for c in range(tq // ATTN_SUB):
        q = q_ref[c * ATTN_SUB:(c + 1) * ATTN_SUB, :]
        qq = jnp.concatenate([jnp.where(lane < HEAD_DIM, q, zero),
                              jnp.where(lane >= HEAD_DIM, q, zero)], axis=0)
        s = _dot_nt(qq, k_scr[...])
        mx = jnp.max(s, axis=-1, keepdims=True)
        p = jnp.exp2((s - mx).astype(BF16))
        oe = _dot(p, v_scr[...])
        o2 = oe[:, 0:V_DIM] / oe[:, V_DIM:2 * V_DIM]
        o = o2[0:ATTN_SUB] - lam * o2[ATTN_SUB:2 * ATTN_SUB]
        o = o * lax.rsqrt(jnp.mean(o * o, axis=-1, keepdims=True) + EPS) * gain
        o_ref[c * ATTN_SUB:(c + 1) * ATTN_SUB, :] = o.astype(o_ref.dtype)


def _attention(lam_p, subln_g, qk, v, kvc, bsz, n_lat, n_ctx):
    tq = min(2048, n_lat)
    nq = n_lat // tq
    t = bsz * n_lat
    sel = lambda which, rows: pl.BlockSpec((None, None, rows, LANES),
                                           lambda b, h, i: (which, h, b, 0))
    return pl.pallas_call(
        _attn_kernel,
        grid=(bsz, N_HEADS, nq),
        scratch_shapes=[pltpu.VMEM((n_lat + n_ctx, LANES), BF16),
                        pltpu.VMEM((n_lat + n_ctx, 2 * V_DIM), BF16)],
        in_specs=[pl.BlockSpec((4, HEAD_DIM), lambda b, h, i: (0, 0)),
                  pl.BlockSpec((1, V_DIM), lambda b, h, i: (0, 0)),
                  pl.BlockSpec((None, None, tq, LANES), lambda b, h, i: (0, h, b * nq + i, 0)),
                  sel(1, n_lat), sel(0, n_lat), sel(0, n_ctx), sel(1, n_ctx)],
        out_specs=pl.BlockSpec((None, tq, LANES), lambda b, h, i: (h, b * nq + i, 0)),
        out_shape=jax.ShapeDtypeStruct((N_HEADS, t, LANES), BF16),
        compiler_params=_cparams("arbitrary", "arbitrary", "arbitrary"),
        name="diff_attn",
    )(lam_p, subln_g, qk, qk, v, kvc, kvc)


def _unchunk_kernel(y_ref, o_ref, nat_ref):
    nch = y_ref.shape[-1]
    for t in range(SSM_CHUNK):
        yt = y_ref[:, t].reshape(D_SSM, nch).astype(F32).T
        for lt in range(D_SSM // LANES):
            nat_ref[lt, pl.ds(t, nch, stride=SSM_CHUNK), :] = yt[:, lt * LANES:(lt + 1) * LANES]
    for lt in range(D_SSM // LANES):
        o_ref[:, lt * LANES:(lt + 1) * LANES] = nat_ref[lt].astype(o_ref.dtype)


def _unchunk(yt, bsz, n_lat):
    nch = n_lat // SSM_CHUNK
    return pl.pallas_call(
        _unchunk_kernel,
        grid=(bsz,),
        in_specs=[pl.BlockSpec((SSM_GROUPS, SSM_CHUNK, GROUP_CH, nch), lambda b: (0, 0, 0, b))],
        out_specs=pl.BlockSpec((n_lat, D_SSM), lambda b: (b, 0)),
        out_shape=jax.ShapeDtypeStruct((bsz * n_lat, D_SSM), BF16),
        scratch_shapes=[pltpu.VMEM((D_SSM // LANES, n_lat, LANES), F32)],
        compiler_params=_cparams("arbitrary"),
        name="unchunk",
    )(yt)


MERGE_SUB = 256


def _merge_kernel(y_ref, o_ref, g_ref, x_ref, mod_ref, wg_ref, bg_ref, wps_ref, wpa_ref, wo_ref,
                  out_ref):
    for r in range(x_ref.shape[0] // MERGE_SUB):
        rows = slice(r * MERGE_SUB, (r + 1) * MERGE_SUB)
        y = y_ref[rows, :].astype(F32)
        z = 0.5 * y * (1.0 + jnp.tanh(math.sqrt(2.0 / math.pi) * (y + 0.044715 * (y * y * y))))
        gl = _dot(z.astype(BF16), wg_ref[...]) + bg_ref[...]
        ys = (z * _sigmoid(gl)).astype(BF16)
        oa = jnp.concatenate([o_ref[h, rows, :] for h in range(N_HEADS)], axis=1)
        merged = (g_ref[rows, 0:D_MODEL].astype(F32) * _dot(ys, wps_ref[...])
                  + g_ref[rows, D_MODEL:2 * D_MODEL].astype(F32) * _dot(oa, wpa_ref[...]))
        out = _dot(merged.astype(BF16), wo_ref[...])
        out_ref[rows, :] = x_ref[rows, :] + mod_ref[2:3, :] * out


def _merge(y, o, gates, x2d, mod, w_glu, b_glu, w_ps, w_pa, w_o, n_lat):
    t = x2d.shape[0]
    tm = min(512, n_lat)
    tiles_per_b = n_lat // tm
    row = lambda i: (i, 0)
    const = lambda shape: pl.BlockSpec(shape, lambda i: (0,) * len(shape),
                                       pipeline_mode=pl.Buffered(1))
    return pl.pallas_call(
        _merge_kernel,
        grid=(t // tm,),
        in_specs=[pl.BlockSpec((tm, D_SSM), row),
                  pl.BlockSpec((N_HEADS, tm, LANES), lambda i: (0, i, 0)),
                  pl.BlockSpec((tm, 2 * D_MODEL), row),
                  pl.BlockSpec((tm, D_MODEL), row),
                  pl.BlockSpec((None, 6, D_MODEL), lambda i: (i // tiles_per_b, 0, 0)),
                  const((D_SSM, D_SSM)), const((1, D_SSM)),
                  const((D_SSM, D_MODEL)), const((D_ATTN, D_MODEL)), const((D_MODEL, D_MODEL))],
        out_specs=pl.BlockSpec((tm, D_MODEL), row),
        out_shape=jax.ShapeDtypeStruct((t, D_MODEL), F32),
        compiler_params=_cparams("arbitrary"),
        name="merge",
    )(y, o, gates, x2d, mod, w_glu, b_glu, w_ps, w_pa, w_o)


FFN_SUB = 256


def _ffn_kernel(x_ref, mod_ref, g2_ref, gf_ref, wg_ref, wu_ref, wo_ref, out_ref, h_ref):
    j = pl.program_id(1)
    last = pl.num_programs(1) - 1

    def chains(first, final):
        for r in range(x_ref.shape[0] // FFN_SUB):
            rows = slice(r * FFN_SUB, (r + 1) * FFN_SUB)
            if first:
                h_ref[rows, :] = _norm_mod(x_ref[rows, :], g2_ref[...], mod_ref[3:4, :],
                                           mod_ref[4:5, :]).astype(BF16)
            h = h_ref[rows, :]
            gate = _dot(h, wg_ref[...])
            up = _dot(h, wu_ref[...])
            act = (gate * _sigmoid(gate) * up).astype(BF16)
            part = _dot(act, wo_ref[...])
            if first:
                out_ref[rows, :] = part
            elif not final:
                out_ref[rows, :] += part
            else:
                xo = x_ref[rows, :] + mod_ref[5:6, :] * (out_ref[rows, :] + part)
                ms = jnp.mean(xo * xo, axis=-1, keepdims=True)
                out_ref[rows, :] = xo * lax.rsqrt(ms + EPS) * gf_ref[...]

    pl.when(j == 0)(lambda: chains(True, False))
    pl.when((j > 0) & (j < last))(lambda: chains(False, False))
    pl.when(j == last)(lambda: chains(False, True))


FFN_TF = 512


def _ffn(x_mid, mod, norm2_g, norm_f_g, w_ffn_in, w_ffn_out, n_lat):
    t = x_mid.shape[0]
    tm = min(1024, n_lat)
    tf = FFN_TF
    nf = D_FF // tf
    assert nf >= 2, "the kernel needs distinct first and last hidden-dimension steps"
    tiles_per_b = n_lat // tm
    row = lambda i, j: (i, 0)
    return pl.pallas_call(
        _ffn_kernel,
        grid=(t // tm, nf),
        in_specs=[pl.BlockSpec((tm, D_MODEL), row),
                  pl.BlockSpec((None, 6, D_MODEL), lambda i, j: (i // tiles_per_b, 0, 0)),
                  pl.BlockSpec((1, D_MODEL), lambda i, j: (0, 0)),
                  pl.BlockSpec((1, D_MODEL), lambda i, j: (0, 0)),
                  pl.BlockSpec((D_MODEL, tf), lambda i, j: (0, j)),
                  pl.BlockSpec((D_MODEL, tf), lambda i, j: (0, j + nf)),
                  pl.BlockSpec((tf, D_MODEL), lambda i, j: (j, 0))],
        out_specs=pl.BlockSpec((tm, D_MODEL), row),
        out_shape=jax.ShapeDtypeStruct((t, D_MODEL), F32),
        scratch_shapes=[pltpu.VMEM((tm, D_MODEL), BF16)],
        compiler_params=_cparams("arbitrary", "arbitrary", vmem_limit=VMEM_LIMIT_FFN),
        name="ffn",
    )(x_mid, mod, norm2_g, norm_f_g, w_ffn_in, w_ffn_in, w_ffn_out)


def kernel(x, c, ctx, c_ctx, w_ada, b_ada, norm1_g, w_in, ssm_a_re, ssm_a_im, ssm_log_dt, ssm_b_re, ssm_b_im, ssm_c_re, ssm_c_im, ssm_d, w_glu, b_glu, lambda_q1, lambda_k1, lambda_q2, lambda_k2, subln_g, w_proj_ssm, w_proj_attn, w_out, norm2_g, w_ffn_in, w_ffn_out, norm_f_g):
    bsz, n_lat, d = x.shape
    n_ctx = ctx.shape[1]
    assert d == D_MODEL and w_ada.shape[0] == 1, "single-layer block only"
    assert n_lat % GRID_W == 0 and n_lat % SSM_CHUNK == 0 and n_ctx % SSM_CHUNK == 0
    assert bsz % SUBLANES == 0

    pad = (-(bsz + 1)) % SUBLANES
    cc = jnp.concatenate([c, c_ctx[None, :], jnp.zeros((pad, d), F32)], axis=0)
    mod_all = _adaln(cc, w_ada[0], b_ada[0])
    mod = mod_all[:bsz].reshape(bsz, 6, d)
    mod_c = mod_all[bsz:bsz + 1].reshape(1, 6, d)

    w_in_b = w_in[0].astype(BF16)
    g1 = norm1_g[0].reshape(1, d)
    x2d = x.reshape(bsz * n_lat, d)
    h, qk, v, gates = _proj_lat(x2d, mod, g1, w_in_b, n_lat)
    hc = _prenorm(ctx.reshape(bsz * n_ctx, d), mod_c, g1, n_ctx)
    kvc = _proj_heads(hc, w_in_b, (D_SSM + D_QK) // PROJ_TN, 2)
    zl = _proj_u(h, w_in_b)
    zc = _proj_u(hc, w_in_b)

    m, w, vv, a = _ssm_params(ssm_a_re[0], ssm_a_im[0], ssm_log_dt[0], ssm_b_re[0], ssm_b_im[0],
                              ssm_c_re[0], ssm_c_im[0], ssm_d[0])
    yt = _ssm_scan(zl, zc, m, w, vv, a, bsz)

    lam_p = jnp.stack([lambda_q1[0], lambda_k1[0], lambda_q2[0], lambda_k2[0]]).astype(F32)
    o = _attention(lam_p, subln_g[0].reshape(1, V_DIM), qk, v, kvc, bsz, n_lat, n_ctx)

    y = _unchunk(yt, bsz, n_lat)
    x_mid = _merge(y, o, gates, x2d, mod, w_glu[0].astype(BF16), b_glu[0].reshape(1, D_SSM),
                   w_proj_ssm[0].astype(BF16), w_proj_attn[0].astype(BF16), w_out[0].astype(BF16),
                   n_lat)
    out = _ffn(x_mid, mod, norm2_g[0].reshape(1, d), norm_f_g.reshape(1, d),
               w_ffn_in[0].astype(BF16), w_ffn_out[0].astype(BF16), n_lat)
    return out.reshape(bsz, n_lat, d)
```

```python
import functools
import math

import jax
import jax.numpy as jnp
from jax import lax
from jax.experimental import pallas as pl
from jax.experimental.pallas import tpu as pltpu

F32 = jnp.float32
BF16 = jnp.bfloat16

D_MODEL = 2048
GRID_W = 64
D_SSM = 1024
GROUP_CH = 16
SSM_GROUPS = D_SSM // GROUP_CH
SSM_STATE = 64
N_HEADS = 8
HEAD_DIM = 64
V_DIM = 2 * HEAD_DIM
D_QK = N_HEADS * 2 * HEAD_DIM
D_ATTN = N_HEADS * V_DIM
ATTN_SCALE = HEAD_DIM ** -0.5
Q_SCALE = ATTN_SCALE * math.log2(math.e)
ROPE_AXIS_DIM = HEAD_DIM // 2
ROPE_BASE = 10000.0
D_FF = 5632
D_IN = D_SSM + 2 * D_QK + D_ATTN + 2 * D_MODEL
EPS = 1e-6
LAM_INIT = 0.8 - 0.6 * math.exp(-0.3 * 0)

LANES = 128
SUBLANES = 8
MXU_DIM = 256
SSM_CHUNK = 16
CHUNK_W = SSM_CHUNK * GROUP_CH
STATE_W = 2 * SSM_STATE
VMEM_LIMIT = 56 * 1024 * 1024
VMEM_LIMIT_BIG = 60 * 1024 * 1024


def _cparams(*sem, vmem_limit=VMEM_LIMIT):
    return pltpu.CompilerParams(dimension_semantics=sem, vmem_limit_bytes=vmem_limit)


def _dot(a, b):
    return jnp.dot(a, b, preferred_element_type=F32)


def _dot_nt(a, b, precision=None):
    return lax.dot_general(a, b, (((1,), (1,)), ((), ())), preferred_element_type=F32,
                           precision=precision)


def _dot_tn(a, b):
    return lax.dot_general(a, b, (((0,), (0,)), ((), ())), preferred_element_type=F32)


def _sigmoid(x):
    return 1.0 / (1.0 + jnp.exp(-x))


def _adaln_kernel(c_ref, w_ref, b_ref, o_ref):
    c = c_ref[...]
    s = (c * _sigmoid(c)).astype(BF16)
    o_ref[...] = _dot(s, w_ref[...].astype(BF16)) + b_ref[...]


def _adaln(cc, w_ada, b_ada):
    rows, d = cc.shape
    n = w_ada.shape[1]
    tn = 1024
    return pl.pallas_call(
        _adaln_kernel,
        grid=(n // tn,),
        in_specs=[pl.BlockSpec((rows, d), lambda j: (0, 0)),
                  pl.BlockSpec((d, tn), lambda j: (0, j)),
                  pl.BlockSpec((1, tn), lambda j: (0, j))],
        out_specs=pl.BlockSpec((rows, tn), lambda j: (0, j)),
        out_shape=jax.ShapeDtypeStruct((rows, n), F32),
        compiler_params=_cparams("arbitrary"),
        name="adaln",
    )(cc, w_ada, b_ada.reshape(1, n))


def _norm_mod(x, g, shift, scale):
    ms = jnp.mean(x * x, axis=-1, keepdims=True)
    y = x * lax.rsqrt(ms + EPS) * g
    return y * (1.0 + scale) + shift


PROJ_TN = 1024


def _rope_tables(n_lat):
    pos = jnp.arange(n_lat, dtype=jnp.int32)
    row = (pos // GRID_W).astype(F32)
    col = (pos % GRID_W).astype(F32)
    inv = 1.0 / (ROPE_BASE ** (jnp.arange(0, ROPE_AXIS_DIM, 2, dtype=F32) / ROPE_AXIS_DIM))
    lane = jnp.arange(LANES)
    d = lane % HEAD_DIM
    freq = inv[d % (ROPE_AXIS_DIM // 2)]
    ang = jnp.where(d < ROPE_AXIS_DIM, row[:, None], col[:, None]) * freq[None, :]
    first = (lane % ROPE_AXIS_DIM) < (ROPE_AXIS_DIM // 2)
    cos, sin = jnp.cos(ang), jnp.sin(ang)
    return cos, jnp.where(first, -sin, 0.0), jnp.where(first, 0.0, sin)


PROJ_SUB = 256
N_LAT_SLABS = (2 * D_QK + D_ATTN + 2 * D_MODEL) // PROJ_TN


def _proj_lat_kernel(x_ref, mod_ref, g_ref, w_ref, cos_ref, sa_ref, sb_ref,
                     h_ref, qk_ref, v_ref, gate_ref, h_scr):
    j = pl.program_id(1)

    def chains(kind):
        for r in range(x_ref.shape[0] // PROJ_SUB):
            rows = slice(r * PROJ_SUB, (r + 1) * PROJ_SUB)
            if kind == "q":
                hr = _norm_mod(x_ref[rows, :], g_ref[...], mod_ref[0:1, :], mod_ref[1:2, :]).astype(BF16)
                h_scr[rows, :] = hr
                h_ref[rows, :] = hr
            acc = _dot(h_scr[rows, :], w_ref[...])
            if kind in ("q", "k"):
                scale = Q_SCALE if kind == "q" else 1.0
                cos, sa, sb = cos_ref[rows, :] * scale, sa_ref[rows, :] * scale, sb_ref[rows, :] * scale
                for hh in range(N_HEADS):
                    a = acc[:, hh * LANES:(hh + 1) * LANES]
                    rot = a * cos + pltpu.roll(a, LANES - 16, 1) * sa + pltpu.roll(a, 16, 1) * sb
                    qk_ref[hh, rows, :] = rot.astype(BF16)
            elif kind == "v":
                _store_v_ext(v_ref, rows, acc)
            else:
                gate_ref[rows, :] = _sigmoid(acc).astype(BF16)

    pl.when(j == 0)(lambda: chains("q"))
    pl.when(j == 1)(lambda: chains("k"))
    pl.when(j == 2)(lambda: chains("v"))
    pl.when(j >= 3)(lambda: chains("gate"))


def _proj_lat(x2d, mod, norm_g, w_in, n_lat):
    t = x2d.shape[0]
    tm = min(1024, n_lat)
    tiles_per_b = n_lat // tm
    tab = pl.BlockSpec((tm, LANES), lambda i, j: (i % tiles_per_b, 0))
    col0 = D_SSM // PROJ_TN
    heads = lambda imap, width=LANES: pl.BlockSpec((None, N_HEADS, tm, width), imap)
    return pl.pallas_call(
        _proj_lat_kernel,
        grid=(t // tm, N_LAT_SLABS),
        in_specs=[pl.BlockSpec((tm, D_MODEL), lambda i, j: (i, 0)),
                  pl.BlockSpec((None, 6, D_MODEL), lambda i, j: (i // tiles_per_b, 0, 0)),
                  pl.BlockSpec((1, D_MODEL), lambda i, j: (0, 0)),
                  pl.BlockSpec((D_MODEL, PROJ_TN), lambda i, j: (0, col0 + j)),
                  tab, tab, tab],
        out_specs=[pl.BlockSpec((tm, D_MODEL), lambda i, j: (i, 0)),
                   heads(lambda i, j: (jnp.minimum(j, 1), 0, i, 0)),
                   heads(lambda i, j: (0, 0, i, 0), 2 * V_DIM),
                   pl.BlockSpec((tm, PROJ_TN), lambda i, j: (i, jnp.maximum(j - 3, 0)))],
        out_shape=[jax.ShapeDtypeStruct((t, D_MODEL), BF16),
                   jax.ShapeDtypeStruct((2, N_HEADS, t, LANES), BF16),
                   jax.ShapeDtypeStruct((1, N_HEADS, t, 2 * V_DIM), BF16),
                   jax.ShapeDtypeStruct((t, 2 * D_MODEL), BF16)],
        scratch_shapes=[pltpu.VMEM((tm, D_MODEL), BF16)],
        compiler_params=_cparams("arbitrary", "arbitrary", vmem_limit=VMEM_LIMIT_BIG),
        name="proj_lat",
    )(x2d, mod, norm_g, w_in, *_rope_tables(n_lat))


def _store_v_ext(v_ref, rows, acc):
    ones = jnp.ones((acc.shape[0], V_DIM), BF16)
    for hh in range(N_HEADS):
        v_ref[hh, rows, 0:V_DIM] = acc[:, hh * LANES:(hh + 1) * LANES].astype(BF16)
        v_ref[hh, rows, V_DIM:2 * V_DIM] = ones


def _proj_ctx_kernel(x_ref, mod_ref, g_ref, w_ref, h_ref, k_ref, v_ref, h_scr):
    j = pl.program_id(1)
    rows = slice(0, x_ref.shape[0])

    @pl.when(j == 0)
    def _():
        hr = _norm_mod(x_ref[...], g_ref[...], mod_ref[0:1, :], mod_ref[1:2, :]).astype(BF16)
        h_scr[...] = hr
        h_ref[...] = hr
        acc = _dot(hr, w_ref[...])
        for hh in range(N_HEADS):
            k_ref[hh] = acc[:, hh * LANES:(hh + 1) * LANES].astype(BF16)

    @pl.when(j == 1)
    def _():
        _store_v_ext(v_ref, rows, _dot(h_scr[...], w_ref[...]))


def _proj_ctx(x2d, mod_c, norm_g, w_in):
    t = x2d.shape[0]
    tm = min(512, t)
    col0 = (D_SSM + D_QK) // PROJ_TN
    return pl.pallas_call(
        _proj_ctx_kernel,
        grid=(t // tm, 2),
        in_specs=[pl.BlockSpec((tm, D_MODEL), lambda i, j: (i, 0)),
                  pl.BlockSpec((None, 6, D_MODEL), lambda i, j: (0, 0, 0)),
                  pl.BlockSpec((1, D_MODEL), lambda i, j: (0, 0)),
                  pl.BlockSpec((D_MODEL, PROJ_TN), lambda i, j: (0, col0 + j))],
        out_specs=[pl.BlockSpec((tm, D_MODEL), lambda i, j: (i, 0)),
                   pl.BlockSpec((N_HEADS, tm, LANES), lambda i, j: (0, i, 0)),
                   pl.BlockSpec((N_HEADS, tm, 2 * V_DIM), lambda i, j: (0, i, 0))],
        out_shape=[jax.ShapeDtypeStruct((t, D_MODEL), BF16),
                   jax.ShapeDtypeStruct((N_HEADS, t, LANES), BF16),
                   jax.ShapeDtypeStruct((N_HEADS, t, 2 * V_DIM), BF16)],
        scratch_shapes=[pltpu.VMEM((tm, D_MODEL), BF16)],
        compiler_params=_cparams("arbitrary", "arbitrary"),
        name="proj_ctx",
    )(x2d, mod_c, norm_g, w_in)


U_ROWS = 512
GROUPS_PER_TILE = LANES // GROUP_CH


def _proj_u_kernel(h_ref, w_ref, z_ref, u_ref):
    n_tok = h_ref.shape[0]
    nch = n_tok // SSM_CHUNK
    for r in range(n_tok // U_ROWS):
        rows = slice(r * U_ROWS, (r + 1) * U_ROWS)
        u = _dot(h_ref[rows, :], w_ref[...])
        for lt in range(D_SSM // LANES):
            u_ref[lt, rows, :] = u[:, lt * LANES:(lt + 1) * LANES]
    for t in range(SSM_CHUNK):
        for lt in range(D_SSM // LANES):
            comb = u_ref[lt, pl.ds(t, nch, stride=SSM_CHUNK), :]
            z_ref[lt * GROUPS_PER_TILE:(lt + 1) * GROUPS_PER_TILE, t] = (
                comb.T.reshape(GROUPS_PER_TILE, GROUP_CH, nch).astype(BF16))


def _proj_u(h, w_u):
    t = h.shape[0]
    tm = min(SSM_CHUNK * LANES, t)
    nch = tm // SSM_CHUNK
    return pl.pallas_call(
        _proj_u_kernel,
        grid=(t // tm,),
        in_specs=[pl.BlockSpec((tm, D_MODEL), lambda i: (i, 0)),
                  pl.BlockSpec((D_MODEL, D_SSM), lambda i: (0, 0), pipeline_mode=pl.Buffered(1))],
        out_specs=pl.BlockSpec((SSM_GROUPS, SSM_CHUNK, GROUP_CH, nch), lambda i: (0, 0, 0, i)),
        out_shape=jax.ShapeDtypeStruct((SSM_GROUPS, SSM_CHUNK, GROUP_CH, t // SSM_CHUNK), BF16),
        scratch_shapes=[pltpu.VMEM((D_SSM // LANES, tm, LANES), F32)],
        compiler_params=_cparams("arbitrary"),
        name="proj_u",
    )(h, w_u)


PARAM_GROUPS = 8


def _ssm_param_kernel(*refs):
    for gi in range(PARAM_GROUPS):
        _ssm_param_group(*[r.at[gi] for r in refs])


def _ssm_param_group(a_re_ref, a_im_ref, ldt_ref, bt_ref, btsw_ref, cr_ref, ci_ref, d_ref,
                     m_ref, w_ref, v_ref, a_ref):
    tc = SSM_CHUNK
    lane = lax.broadcasted_iota(jnp.int32, (1, STATE_W), 1)
    lo = lane < SSM_STATE
    kts = []
    for d in range(2):
        dt = jnp.exp(ldt_ref[d:d + 1, :])
        lre = a_re_ref[d:d + 1, :] * dt
        lim = a_im_ref[d:d + 1, :] * dt
        kk = lax.broadcasted_iota(jnp.int32, (tc + 1, STATE_W), 0).astype(F32)
        mag = jnp.exp(kk * lre)
        c2 = mag * jnp.cos(kk * lim)
        s2 = mag * jnp.sin(kk * lim)
        xr = c2[1:2] - 1.0
        xi = s2[1:2]
        ar = a_re_ref[d:d + 1, :]
        ai = a_im_ref[d:d + 1, :]
        den = ar * ar + ai * ai
        er = (xr * ar + xi * ai) / den
        ei = (xi * ar - xr * ai) / den
        e2 = jnp.where(lo, -ei, ei)
        bt = bt_ref[d]
        btsw = btsw_ref[d]
        bb = er * bt + e2 * btsw
        bbsw = er * btsw - e2 * bt
        cdr = cr_ref[d]
        cdi = ci_ref[d]
        cneg = jnp.where(lo, cdr, -cdi)
        p2 = jnp.where(lo, -s2, s2)
        q1 = jnp.where(lo, c2, -s2)
        q2 = jnp.where(lo, -s2, -c2)
        gl = []
        for s in range(tc):
            kw = tc - 1 - s if d == 0 else s
            wd = c2[kw:kw + 1] * bb + p2[kw:kw + 1] * bbsw
            wsw = c2[kw:kw + 1] * bbsw - p2[kw:kw + 1] * bb
            w_ref[s, :, (2 * d) * STATE_W:(2 * d + 1) * STATE_W] = wd.astype(w_ref.dtype)
            w_ref[s, :, (2 * d + 1) * STATE_W:(2 * d + 2) * STATE_W] = wsw.astype(w_ref.dtype)
            kv = s + 1 if d == 0 else tc - s
            v_ref[s, :, d * STATE_W:(d + 1) * STATE_W] = (
                cdr * q1[kv:kv + 1] + cdi * q2[kv:kv + 1]).astype(v_ref.dtype)
            kl = tc - 1 - s if d == 0 else s
            gl.append(c2[kl:kl + 1] * bb + p2[kl:kl + 1] * bbsw)
        gcat = jnp.concatenate(gl, axis=0)
        kts.append(_dot_nt(cneg, gcat, precision=lax.Precision.HIGHEST))
        a_ref[d, 0:1, :] = c2[tc:tc + 1]
        a_ref[d, 1:2, :] = p2[tc:tc + 1]
    lane_w = lax.broadcasted_iota(jnp.int32, (GROUP_CH, CHUNK_W), 1)
    row_w = lax.broadcasted_iota(jnp.int32, (GROUP_CH, CHUNK_W), 0)
    dcol = d_ref[...]
    for t in range(tc):
        sh = (CHUNK_W - GROUP_CH * (tc - 1 - t)) % CHUNK_W
        fwd = kts[0] if sh == 0 else pltpu.roll(kts[0], sh, 1)
        fwd = jnp.where(lane_w < GROUP_CH * (t + 1), fwd, 0.0)
        rev = kts[1] if t == 0 else pltpu.roll(kts[1], GROUP_CH * t, 1)
        rev = jnp.where(lane_w >= GROUP_CH * t, rev, 0.0)
        skip = jnp.where(lane_w == GROUP_CH * t + row_w, dcol, 0.0)
        m_ref[t] = (fwd + rev + skip).astype(m_ref.dtype)


def _ssm_params(a_re, a_im, log_dt, b_re, b_im, c_re, c_im, d_skip):
    g = SSM_GROUPS
    dup = lambda a: jnp.concatenate([a, a], axis=-1)
    a_re2 = dup(jnp.transpose(a_re, (1, 0, 2)))
    a_im2 = dup(jnp.transpose(a_im, (1, 0, 2)))
    ldt = jnp.transpose(log_dt, (1, 0))[:, :, None]
    btr = jnp.transpose(b_re, (1, 0, 3, 2))
    bti = jnp.transpose(b_im, (1, 0, 3, 2))
    bt = jnp.concatenate([btr, bti], axis=-1)
    btsw = jnp.concatenate([bti, btr], axis=-1)
    cdr = dup(jnp.transpose(c_re, (1, 0, 2, 3)))
    cdi = dup(jnp.transpose(c_im, (1, 0, 2, 3)))
    dcol = d_skip.reshape(g, GROUP_CH, 1)
    pg = PARAM_GROUPS
    g3 = lambda n: pl.BlockSpec((pg, 2, n), lambda i: (i, 0, 0))
    g4 = lambda r, n: pl.BlockSpec((pg, 2, r, n), lambda i: (i, 0, 0, 0))
    tc = SSM_CHUNK
    out = lambda n: pl.BlockSpec((pg, tc, GROUP_CH, n), lambda i: (i, 0, 0, 0))
    m, w, v, a = pl.pallas_call(
        _ssm_param_kernel,
        grid=(g // pg,),
        in_specs=[g3(STATE_W), g3(STATE_W), g3(1),
                  g4(GROUP_CH, STATE_W), g4(GROUP_CH, STATE_W),
                  g4(GROUP_CH, STATE_W), g4(GROUP_CH, STATE_W),
                  pl.BlockSpec((pg, GROUP_CH, 1), lambda i: (i, 0, 0))],
        out_specs=[out(CHUNK_W), out(4 * STATE_W), out(2 * STATE_W),
                   pl.BlockSpec((pg, 2, 2, STATE_W), lambda i: (i, 0, 0, 0))],
        out_shape=[jax.ShapeDtypeStruct((g, tc, GROUP_CH, CHUNK_W), BF16),
                   jax.ShapeDtypeStruct((g, tc, GROUP_CH, 4 * STATE_W), BF16),
                   jax.ShapeDtypeStruct((g, tc, GROUP_CH, 2 * STATE_W), BF16),
                   jax.ShapeDtypeStruct((g, 2, 2, STATE_W), F32)],
        compiler_params=_cparams("arbitrary"),
        name="ssm_params",
    )(a_re2, a_im2, ldt, bt, btsw, cdr, cdi, dcol)
    return (m.reshape(g, CHUNK_W, CHUNK_W), w.reshape(g, CHUNK_W, 4 * STATE_W),
            v.reshape(g, CHUNK_W, 2 * STATE_W), a)


ROW_PAD = SUBLANES


SCAN_GROUPS = 2


def _ssm_kernel(bsz, zl_ref, zc_ref, m_ref, w_ref, v_ref, a_ref, y_ref, sl_ref, sc_ref, h_ref):
    ng = SCAN_GROUPS
    ncl = zl_ref.shape[-1] // bsz
    ncc = zc_ref.shape[-1] // bsz
    pl_l = ncl + ROW_PAD
    pl_c = ncc + ROW_PAD
    zls = [zl_ref[g].reshape(CHUNK_W, bsz * ncl) for g in range(ng)]
    for g in range(ng):
        w = w_ref[g]
        s_c = _dot_tn(zc_ref[g].reshape(CHUNK_W, bsz * ncc), w)
        for b in range(bsz):
            s_l = _dot_tn(zls[g][:, b * ncl:(b + 1) * ncl], w)
            for k in range(4):
                sl_ref[4 * g + k, b * pl_l:b * pl_l + ncl, :] = s_l[:, k * STATE_W:(k + 1) * STATE_W]
                sc_ref[4 * g + k, b * pl_c:b * pl_c + ncc, :] = s_c[b * ncc:(b + 1) * ncc,
                                                                    k * STATE_W:(k + 1) * STATE_W]

    a1 = [(a_ref[g, 0, 0:1, :], a_ref[g, 1, 0:1, :]) for g in range(ng)]
    a2 = [(a_ref[g, 0, 1:2, :], a_ref[g, 1, 1:2, :]) for g in range(ng)]

    def step(g, d, h, hsw, s, ssw):
        return a1[g][d] * h + a2[g][d] * hsw + s, a1[g][d] * hsw - a2[g][d] * h + ssw

    def body(s_ref, n, pitch, store):
        def run(j, hs):
            rf = pl.ds(j, bsz, stride=pitch)
            rr = pl.ds(n - 1 - j, bsz, stride=pitch)
            out = []
            for g in range(ng):
                hf, hfs, hr, hrs = hs[4 * g:4 * g + 4]
                if store:
                    h_ref[2 * g, rf, :] = hf
                    h_ref[2 * g + 1, rr, :] = hr
                hf, hfs = step(g, 0, hf, hfs, s_ref[4 * g, rf, :], s_ref[4 * g + 1, rf, :])
                hr, hrs = step(g, 1, hr, hrs, s_ref[4 * g + 2, rr, :], s_ref[4 * g + 3, rr, :])
                out += [hf, hfs, hr, hrs]
            return tuple(out)
        return run

    zero = jnp.zeros((bsz, STATE_W), F32)
    hs = lax.fori_loop(0, ncc, body(sc_ref, ncc, pl_c, False), (zero,) * (4 * ng))
    lax.fori_loop(0, ncl, body(sl_ref, ncl, pl_l, True), hs)
    for g in range(ng):
        hcat = jnp.concatenate(
            [jnp.concatenate([h_ref[2 * g, b * pl_l:b * pl_l + ncl, :],
                              h_ref[2 * g + 1, b * pl_l:b * pl_l + ncl, :]], axis=1)
             for b in range(bsz)], axis=0).astype(BF16)
        y = _dot(m_ref[g], zls[g]) + _dot_nt(v_ref[g], hcat)
        y_ref[g] = y.reshape(SSM_CHUNK, GROUP_CH, bsz * ncl).astype(y_ref.dtype)


def _ssm_scan(zl, zc, m, w, v, a, bsz):
    g = zl.shape[0]
    nl = zl.shape[-1]
    nc = zc.shape[-1]
    ng = SCAN_GROUPS
    blk = lambda *s: pl.BlockSpec((ng,) + s, lambda i: (i,) + (0,) * len(s))
    return pl.pallas_call(
        functools.partial(_ssm_kernel, bsz),
        grid=(g // ng,),
        in_specs=[blk(SSM_CHUNK, GROUP_CH, nl), blk(SSM_CHUNK, GROUP_CH, nc), blk(CHUNK_W, CHUNK_W),
                  blk(CHUNK_W, 4 * STATE_W), blk(CHUNK_W, 2 * STATE_W), blk(2, 2, STATE_W)],
        out_specs=blk(SSM_CHUNK, GROUP_CH, nl),
        out_shape=jax.ShapeDtypeStruct((g, SSM_CHUNK, GROUP_CH, nl), BF16),
        scratch_shapes=[pltpu.VMEM((4 * ng, nl + bsz * ROW_PAD, STATE_W), F32),
                        pltpu.VMEM((4 * ng, nc + bsz * ROW_PAD, STATE_W), F32),
                        pltpu.VMEM((2 * ng, nl + bsz * ROW_PAD, STATE_W), F32)],
        compiler_params=_cparams("arbitrary"),
        name="ssm_scan",
    )(zl, zc, m, w, v, a)


ATTN_SUB = 64


ATTN_HEADS = 2


def _attn_kernel(lam_ref, g_ref, q_ref, kl_ref, vl_ref, kc_ref, vc_ref, o_ref):
    tq = q_ref.shape[1]
    lp = lam_ref[...]
    lam = (jnp.exp(jnp.sum(lp[0:1] * lp[1:2], axis=1, keepdims=True))
           - jnp.exp(jnp.sum(lp[2:3] * lp[3:4], axis=1, keepdims=True)) + LAM_INIT)
    gain = g_ref[...] * (1.0 - LAM_INIT)
    lane = lax.broadcasted_iota(jnp.int32, (ATTN_SUB, LANES), 1)
    zero = jnp.zeros((ATTN_SUB, LANES), BF16)
    for hh in range(ATTN_HEADS):
        for c in range(tq // ATTN_SUB):
            q = q_ref[hh, c * ATTN_SUB:(c + 1) * ATTN_SUB, :]
            qq = jnp.concatenate([jnp.where(lane < HEAD_DIM, q, zero),
                                  jnp.where(lane >= HEAD_DIM, q, zero)], axis=0)
            s_l = _dot_nt(qq, kl_ref[hh])
            s_c = _dot_nt(qq, kc_ref[hh])
            mx = jnp.maximum(jnp.max(s_l, axis=-1, keepdims=True), jnp.max(s_c, axis=-1, keepdims=True))
            p_l = jnp.exp2((s_l - mx).astype(BF16))
            p_c = jnp.exp2((s_c - mx).astype(BF16))
            oe = _dot(p_l, vl_ref[hh]) + _dot(p_c, vc_ref[hh])
            o2 = oe[:, 0:V_DIM] / oe[:, V_DIM:2 * V_DIM]
            o = o2[0:ATTN_SUB] - lam * o2[ATTN_SUB:2 * ATTN_SUB]
            o = o * lax.rsqrt(jnp.mean(o * o, axis=-1, keepdims=True) + EPS) * gain
            o_ref[hh, c * ATTN_SUB:(c + 1) * ATTN_SUB, :] = o.astype(o_ref.dtype)


def _attention(lam_p, subln_g, qk, v, kc, vc, bsz, n_lat, n_ctx):
    t = bsz * n_lat
    hp = ATTN_HEADS
    lat = lambda which, width: pl.BlockSpec((None, hp, n_lat, width), lambda b, h: (which, h, b, 0))
    ctx = lambda width: pl.BlockSpec((hp, n_ctx, width), lambda b, h: (h, b, 0))
    return pl.pallas_call(
        _attn_kernel,
        grid=(bsz, N_HEADS // hp),
        in_specs=[pl.BlockSpec((4, HEAD_DIM), lambda b, h: (0, 0)),
                  pl.BlockSpec((1, V_DIM), lambda b, h: (0, 0)),
                  lat(0, LANES), lat(1, LANES), lat(0, 2 * V_DIM), ctx(LANES), ctx(2 * V_DIM)],
        out_specs=pl.BlockSpec((hp, n_lat, LANES), lambda b, h: (h, b, 0)),
        out_shape=jax.ShapeDtypeStruct((N_HEADS, t, LANES), BF16),
        compiler_params=_cparams("arbitrary", "arbitrary"),
        name="diff_attn",
    )(lam_p, subln_g, qk, qk, v, kc, vc)


def _unchunk_kernel(y_ref, o_ref, nat_ref):
    nch = y_ref.shape[-1]
    for t in range(SSM_CHUNK):
        yt = y_ref[:, t].reshape(D_SSM, nch).astype(F32).T
        for lt in range(D_SSM // LANES):
            nat_ref[lt, pl.ds(t, nch, stride=SSM_CHUNK), :] = yt[:, lt * LANES:(lt + 1) * LANES]
    for lt in range(D_SSM // LANES):
        o_ref[:, lt * LANES:(lt + 1) * LANES] = nat_ref[lt].astype(o_ref.dtype)


def _unchunk(yt, bsz, n_lat):
    nch = n_lat // SSM_CHUNK
    return pl.pallas_call(
        _unchunk_kernel,
        grid=(bsz,),
        in_specs=[pl.BlockSpec((SSM_GROUPS, SSM_CHUNK, GROUP_CH, nch), lambda b: (0, 0, 0, b))],
        out_specs=pl.BlockSpec((n_lat, D_SSM), lambda b: (b, 0)),
        out_shape=jax.ShapeDtypeStruct((bsz * n_lat, D_SSM), BF16),
        scratch_shapes=[pltpu.VMEM((D_SSM // LANES, n_lat, LANES), F32)],
        compiler_params=_cparams("arbitrary"),
        name="unchunk",
    )(yt)


MERGE_SUB = 256


def _merge_kernel(y_ref, o_ref, g_ref, x_ref, mod_ref, wg_ref, bg_ref, wps_ref, wpa_ref, wo_ref,
                  out_ref):
    for r in range(x_ref.shape[0] // MERGE_SUB):
        rows = slice(r * MERGE_SUB, (r + 1) * MERGE_SUB)
        y = y_ref[rows, :].astype(F32)
        z = 0.5 * y * (1.0 + jnp.tanh(math.sqrt(2.0 / math.pi) * (y + 0.044715 * (y * y * y))))
        gl = _dot(z.astype(BF16), wg_ref[...]) + bg_ref[...]
        ys = (z * _sigmoid(gl)).astype(BF16)
        oa = jnp.concatenate([o_ref[h, rows, :] for h in range(N_HEADS)], axis=1)
        merged = (g_ref[rows, 0:D_MODEL].astype(F32) * _dot(ys, wps_ref[...])
                  + g_ref[rows, D_MODEL:2 * D_MODEL].astype(F32) * _dot(oa, wpa_ref[...]))
        out = _dot(merged.astype(BF16), wo_ref[...])
        out_ref[rows, :] = x_ref[rows, :] + mod_ref[2:3, :] * out


def _merge(y, o, gates, x2d, mod, w_glu, b_glu, w_ps, w_pa, w_o, n_lat):
    t = x2d.shape[0]
    tm = min(512, n_lat)
    tiles_per_b = n_lat // tm
    row = lambda i: (i, 0)
    const = lambda shape: pl.BlockSpec(shape, lambda i: (0,) * len(shape),
                                       pipeline_mode=pl.Buffered(1))
    return pl.pallas_call(
        _merge_kernel,
        grid=(t // tm,),
        in_specs=[pl.BlockSpec((tm, D_SSM), row),
                  pl.BlockSpec((N_HEADS, tm, LANES), lambda i: (0, i, 0)),
                  pl.BlockSpec((tm, 2 * D_MODEL), row),
                  pl.BlockSpec((tm, D_MODEL), row),
                  pl.BlockSpec((None, 6, D_MODEL), lambda i: (i // tiles_per_b, 0, 0)),
                  const((D_SSM, D_SSM)), const((1, D_SSM)),
                  const((D_SSM, D_MODEL)), const((D_ATTN, D_MODEL)), const((D_MODEL, D_MODEL))],
        out_specs=pl.BlockSpec((tm, D_MODEL), row),
        out_shape=jax.ShapeDtypeStruct((t, D_MODEL), F32),
        compiler_params=_cparams("arbitrary"),
        name="merge",
    )(y, o, gates, x2d, mod, w_glu, b_glu, w_ps, w_pa, w_o)


FFN_SUB = 256


def _ffn_kernel(x_ref, mod_ref, g2_ref, gf_ref, wg_ref, wu_ref, wo_ref, out_ref, h_ref):
    j = pl.program_id(1)
    last = pl.num_programs(1) - 1

    def chains(first, final):
        for r in range(x_ref.shape[0] // FFN_SUB):
            rows = slice(r * FFN_SUB, (r + 1) * FFN_SUB)
            if first:
                h_ref[rows, :] = _norm_mod(x_ref[rows, :], g2_ref[...], mod_ref[3:4, :],
                                           mod_ref[4:5, :]).astype(BF16)
            h = h_ref[rows, :]
            gate = _dot(h, wg_ref[...])
            up = _dot(h, wu_ref[...])
            act = (gate * _sigmoid(gate) * up).astype(BF16)
            part = _dot(act, wo_ref[...])
            if first:
                out_ref[rows, :] = part
            elif not final:
                out_ref[rows, :] += part
            else:
                xo = x_ref[rows, :] + mod_ref[5:6, :] * (out_ref[rows, :] + part)
                ms = jnp.mean(xo * xo, axis=-1, keepdims=True)
                out_ref[rows, :] = xo * lax.rsqrt(ms + EPS) * gf_ref[...]

    pl.when(j == 0)(lambda: chains(True, False))
    pl.when((j > 0) & (j < last))(lambda: chains(False, False))
    pl.when(j == last)(lambda: chains(False, True))


FFN_TF = 512


def _ffn(x_mid, mod, norm2_g, norm_f_g, w_ffn_in, w_ffn_out, n_lat):
    t = x_mid.shape[0]
    tm = min(1024, n_lat)
    tf = FFN_TF
    nf = D_FF // tf
    assert nf >= 2, "the kernel needs distinct first and last hidden-dimension steps"
    tiles_per_b = n_lat // tm
    row = lambda i, j: (i, 0)
    return pl.pallas_call(
        _ffn_kernel,
        grid=(t // tm, nf),
        in_specs=[pl.BlockSpec((tm, D_MODEL), row),
                  pl.BlockSpec((None, 6, D_MODEL), lambda i, j: (i // tiles_per_b, 0, 0)),
                  pl.BlockSpec((1, D_MODEL), lambda i, j: (0, 0)),
                  pl.BlockSpec((1, D_MODEL), lambda i, j: (0, 0)),
                  pl.BlockSpec((D_MODEL, tf), lambda i, j: (0, j)),
                  pl.BlockSpec((D_MODEL, tf), lambda i, j: (0, j + nf)),
                  pl.BlockSpec((tf, D_MODEL), lambda i, j: (j, 0))],
        out_specs=pl.BlockSpec((tm, D_MODEL), row),
        out_shape=jax.ShapeDtypeStruct((t, D_MODEL), F32),
        scratch_shapes=[pltpu.VMEM((tm, D_MODEL), BF16)],
        compiler_params=_cparams("arbitrary", "arbitrary", vmem_limit=VMEM_LIMIT_BIG),
        name="ffn",
    )(x_mid, mod, norm2_g, norm_f_g, w_ffn_in, w_ffn_in, w_ffn_out)


def kernel(x, c, ctx, c_ctx, w_ada, b_ada, norm1_g, w_in, ssm_a_re, ssm_a_im, ssm_log_dt, ssm_b_re, ssm_b_im, ssm_c_re, ssm_c_im, ssm_d, w_glu, b_glu, lambda_q1, lambda_k1, lambda_q2, lambda_k2, subln_g, w_proj_ssm, w_proj_attn, w_out, norm2_g, w_ffn_in, w_ffn_out, norm_f_g):
    bsz, n_lat, d = x.shape
    n_ctx = ctx.shape[1]
    assert d == D_MODEL and w_ada.shape[0] == 1, "single-layer block only"
    assert n_lat % GRID_W == 0 and n_lat % SSM_CHUNK == 0 and n_ctx % SSM_CHUNK == 0
    assert bsz % SUBLANES == 0

    pad = (-(bsz + 1)) % SUBLANES
    cc = jnp.concatenate([c, c_ctx[None, :], jnp.zeros((pad, d), F32)], axis=0)
    mod_all = _adaln(cc, w_ada[0], b_ada[0])
    mod = mod_all[:bsz].reshape(bsz, 6, d)
    mod_c = mod_all[bsz:bsz + 1].reshape(1, 6, d)

    w_in_b = w_in[0].astype(BF16)
    g1 = norm1_g[0].reshape(1, d)
    x2d = x.reshape(bsz * n_lat, d)
    h, qk, v, gates = _proj_lat(x2d, mod, g1, w_in_b, n_lat)
    hc, kc, vc = _proj_ctx(ctx.reshape(bsz * n_ctx, d), mod_c, g1, w_in_b)
    zl = _proj_u(h, w_in_b)
    zc = _proj_u(hc, w_in_b)

    m, w, vv, a = _ssm_params(ssm_a_re[0], ssm_a_im[0], ssm_log_dt[0], ssm_b_re[0], ssm_b_im[0],
                              ssm_c_re[0], ssm_c_im[0], ssm_d[0])
    yt = _ssm_scan(zl, zc, m, w, vv, a, bsz)

    lam_p = jnp.stack([lambda_q1[0], lambda_k1[0], lambda_q2[0], lambda_k2[0]]).astype(F32)
    o = _attention(lam_p, subln_g[0].reshape(1, V_DIM), qk, v, kc, vc, bsz, n_lat, n_ctx)

    y = _unchunk(yt, bsz, n_lat)
    x_mid = _merge(y, o, gates, x2d, mod, w_glu[0].astype(BF16), b_glu[0].reshape(1, D_SSM),
                   w_proj_ssm[0].astype(BF16), w_proj_attn[0].astype(BF16), w_out[0].astype(BF16),
                   n_lat)
    out = _ffn(x_mid, mod, norm2_g[0].reshape(1, d), norm_f_g.reshape(1, d),
               w_ffn_in[0].astype(BF16), w_ffn_out[0].astype(BF16), n_lat)
    return out.reshape(bsz, n_lat, d)
```
